```python
import jax, jax.numpy as jnp
from jax import lax
import numpy as np

D_MODEL = 2048
BATCH = 4
SEQ = 8192
DEPTH = 4

MLSTM_WIDTH = D_MODEL // 2
MLSTM_HEADS = 4
MLSTM_DV = MLSTM_WIDTH // MLSTM_HEADS
MLSTM_DQK = MLSTM_DV // 2
MLSTM_CONV = 4
MLSTM_CHUNK = 64
GATE_SOFTCAP = 15.0

ATTN_WIDTH = D_MODEL - MLSTM_WIDTH
ATTN_HEAD_DIM = 64
ATTN_HEADS = ATTN_WIDTH // ATTN_HEAD_DIM
ATTN_KV_HEADS = max(1, ATTN_HEADS // 8)
WINDOW = 128
ROPE_DIM = ATTN_HEAD_DIM // 4
ROPE_THETA = 500000.0

D_FF = ((11 * D_MODEL // 4) + 255) // 256 * 256
FFN_CONV = 3
NORM_EPS = 1e-6

MLSTM_QK_WIDTH = MLSTM_HEADS * MLSTM_DQK
ATTN_KV_WIDTH = ATTN_KV_HEADS * ATTN_HEAD_DIM
IN_WIDTHS = (MLSTM_QK_WIDTH, MLSTM_QK_WIDTH, MLSTM_WIDTH, MLSTM_WIDTH, MLSTM_HEADS, MLSTM_HEADS,
             ATTN_WIDTH, ATTN_KV_WIDTH, ATTN_KV_WIDTH)
IN_WIDTH = sum(IN_WIDTHS)

kernel_name = 'hymba_style_mlstm_swa_convglu_adaln'


def rms_norm(x, g):
    xf = x.astype(jnp.float32)
    out = xf * lax.rsqrt(jnp.mean(xf * xf, axis=-1, keepdims=True) + NORM_EPS)
    return (out * g.astype(jnp.float32)).astype(x.dtype)


def modulate(h, shift, scale):
    return h * (1.0 + scale[:, None, :]) + shift[:, None, :]


def causal_depthwise_conv(x, w):
    W = w.shape[0]
    S = x.shape[1]
    xp = jnp.pad(x, ((0, 0), (W - 1, 0), (0, 0)))
    w = w.astype(x.dtype)
    out = xp[:, 0:S] * w[0]
    for k in range(1, W):
        out = out + xp[:, k:k + S] * w[k]
    return out


def rope_tables(positions):
    inv_freq = ROPE_THETA ** (-jnp.arange(0, ROPE_DIM, 2, dtype=jnp.float32) / ROPE_DIM)
    ang = positions.astype(jnp.float32)[..., None] * inv_freq
    ang = jnp.concatenate([ang, ang], axis=-1)[:, :, None, :]
    return jnp.cos(ang), jnp.sin(ang)


def apply_partial_rope(t, cos, sin):
    tr, tp = t[..., :ROPE_DIM], t[..., ROPE_DIM:]
    t1, t2 = jnp.split(tr, 2, axis=-1)
    rot = jnp.concatenate([-t2, t1], axis=-1)
    tr = (tr.astype(jnp.float32) * cos + rot.astype(jnp.float32) * sin).astype(t.dtype)
    return jnp.concatenate([tr, tp], axis=-1)


def mlstm_chunkwise(q, k, v, i_pre, f_pre):
    B, S, H, DK = q.shape
    DV = v.shape[-1]
    L = MLSTM_CHUNK
    NC = S // L
    f32 = jnp.float32

    def to_chunks(t):
        t = t.astype(f32).reshape((B, NC, L, H) + t.shape[3:])
        return jnp.moveaxis(t, (1, 3), (0, 2))

    qc = to_chunks(q) * (DK ** -0.5)
    kc = to_chunks(k)
    vc = to_chunks(v)
    ic = to_chunks(i_pre)
    fc = to_chunks(jax.nn.log_sigmoid(f_pre.astype(f32)))
    causal = jnp.tril(jnp.ones((L, L), dtype=bool))

    def step(carry, xs):
        C, n, m = carry
        qj, kj, vj, ij, lf = xs
        b = jnp.cumsum(lf, axis=-1)
        d = b[..., :, None] - b[..., None, :] + ij[..., None, :]
        d = jnp.where(causal, d, -jnp.inf)
        m_inter = b + m[..., None]
        m_t = jnp.maximum(m_inter, d.max(axis=-1))
        w_intra = jnp.exp(d - m_t[..., None])
        w_inter = jnp.exp(m_inter - m_t)
        s = jnp.einsum('bhjd,bhrd->bhjr', qj, kj) * w_intra
        num = (jnp.einsum('bhjr,bhrv->bhjv', s, vj)
               + w_inter[..., None] * jnp.einsum('bhvd,bhjd->bhjv', C, qj))
        nq = s.sum(axis=-1) + w_inter * jnp.einsum('bhd,bhjd->bhj', n, qj)
        h = num / jnp.maximum(jnp.abs(nq), jnp.exp(-m_t))[..., None]
        b_last = b[..., -1]
        a = b_last[..., None] - b + ij
        m_new = jnp.maximum(b_last + m, a.max(axis=-1))
        w_state = jnp.exp(a - m_new[..., None])
        decay = jnp.exp(b_last + m - m_new)
        C_new = decay[..., None, None] * C + jnp.einsum('bhr,bhrv,bhrd->bhvd', w_state, vj, kj)
        n_new = decay[..., None] * n + jnp.einsum('bhr,bhrd->bhd', w_state, kj)
        return (C_new, n_new, m_new), h

    init = (jnp.zeros((B, H, DV, DK), f32), jnp.zeros((B, H, DK), f32), jnp.zeros((B, H), f32))
    _, hc = lax.scan(step, init, (qc, kc, vc, ic, fc))
    return jnp.moveaxis(hc, (0, 2), (1, 3)).reshape(B, S, H, DV).astype(v.dtype)


def sliding_window_attention(q, k, v, sinks):
    B, S, H, Dh = q.shape
    KVH = k.shape[2]
    G = H // KVH
    NB = S // WINDOW
    qb = q.reshape(B, NB, WINDOW, KVH, G, Dh)

    def band(t):
        tb = t.reshape(B, NB, WINDOW, KVH, Dh)
        prev = jnp.concatenate([jnp.zeros_like(tb[:, :1]), tb[:, :-1]], axis=1)
        return jnp.concatenate([prev, tb], axis=2)

    kb, vb = band(k), band(v)
    scores = jnp.einsum('bnqhgd,bnkhd->bnhgqk', qb, kb,
                        preferred_element_type=jnp.float32) * (Dh ** -0.5)
    qi = jnp.arange(WINDOW)[:, None]
    ki = jnp.arange(2 * WINDOW)[None, :]
    in_band = (ki > qi) & (ki <= qi + WINDOW)
    has_prev = (jnp.arange(NB) > 0)[:, None, None] | (ki >= WINDOW)[None]
    mask = in_band[None] & has_prev
    scores = jnp.where(mask[None, :, None, None], scores, -jnp.inf)
    sink = sinks.astype(jnp.float32).reshape(1, 1, KVH, G, 1, 1)
    m = jnp.maximum(scores.max(axis=-1, keepdims=True), sink)
    p = jnp.exp(scores - m)
    probs = p / (p.sum(axis=-1, keepdims=True) + jnp.exp(sink - m))
    out = jnp.einsum('bnhgqk,bnkhd->bnqhgd', probs.astype(v.dtype), vb)
    return out.reshape(B, S, H * Dh)


def hybrid_layer(x, mod, cos, sin, norm_mix_g, w_in, b_gates, conv_qk, mlstm_head_g, sinks,
                 w_out, norm_ffn_g, w_up, conv_ffn, w_down):
    B, S, _ = x.shape
    shift_m, scale_m, gate_m, shift_f, scale_f, gate_f = jnp.split(mod, 6, axis=-1)

    h = modulate(rms_norm(x, norm_mix_g), shift_m, scale_m)
    proj = h @ w_in
    split_points = np.cumsum(IN_WIDTHS)[:-1].tolist()
    q_m, k_m, v_m, o_m, i_pre, f_pre, q_a, k_a, v_a = jnp.split(proj, split_points, axis=-1)

    qk_m = jax.nn.silu(causal_depthwise_conv(jnp.concatenate([q_m, k_m], axis=-1), conv_qk))
    q_m, k_m = jnp.split(qk_m, 2, axis=-1)
    gates = jnp.concatenate([i_pre, f_pre], axis=-1).astype(jnp.float32) + b_gates.astype(jnp.float32)
    gates = GATE_SOFTCAP * jnp.tanh(gates / GATE_SOFTCAP)
    i_pre, f_pre = jnp.split(gates, 2, axis=-1)
    h_m = mlstm_chunkwise(q_m.reshape(B, S, MLSTM_HEADS, MLSTM_DQK),
                          k_m.reshape(B, S, MLSTM_HEADS, MLSTM_DQK),
                          v_m.reshape(B, S, MLSTM_HEADS, MLSTM_DV), i_pre, f_pre)
    h_m = rms_norm(h_m, mlstm_head_g.reshape(MLSTM_HEADS, MLSTM_DV))
    y_m = jax.nn.sigmoid(o_m) * h_m.reshape(B, S, MLSTM_WIDTH)

    q_a = apply_partial_rope(q_a.reshape(B, S, ATTN_HEADS, ATTN_HEAD_DIM), cos, sin)
    k_a = apply_partial_rope(k_a.reshape(B, S, ATTN_KV_HEADS, ATTN_HEAD_DIM), cos, sin)
    v_a = v_a.reshape(B, S, ATTN_KV_HEADS, ATTN_HEAD_DIM)
    y_a = sliding_window_attention(q_a, k_a, v_a, sinks)

    y = jnp.concatenate([y_m, y_a], axis=-1) @ w_out
    x = x + gate_m[:, None, :] * y

    h = modulate(rms_norm(x, norm_ffn_g), shift_f, scale_f)
    g, u = jnp.split(h @ w_up, 2, axis=-1)
    g = causal_depthwise_conv(g, conv_ffn)
    y = (jax.nn.silu(g) * u) @ w_down
    return x + gate_f[:, None, :] * y


def setup_inputs(seed: int = 0) -> dict:
    key = jax.random.key(seed)
    ks = jax.random.split(key, 20)
    f32 = jnp.float32

    def nrm(k, shape, scale):
        return jax.random.normal(k, shape, f32) * scale

    x = nrm(ks[0], (BATCH, SEQ, D_MODEL), 1.0)
    c = nrm(ks[1], (BATCH, D_MODEL), 1.0)
    positions = (jax.random.randint(ks[2], (BATCH, 1), 0, 4096, dtype=jnp.int32)
                 + jnp.arange(SEQ, dtype=jnp.int32)[None, :])
    ada_w = nrm(ks[3], (DEPTH, D_MODEL, 6 * D_MODEL), 0.5 * D_MODEL ** -0.5)
    ada_b = nrm(ks[4], (DEPTH, 6 * D_MODEL), 0.02)
    norm_mix_g = 1.0 + nrm(ks[5], (DEPTH, D_MODEL), 0.02)
    w_in = nrm(ks[6], (DEPTH, D_MODEL, IN_WIDTH), D_MODEL ** -0.5)
    f_bias = jnp.linspace(3.0, 6.0, MLSTM_HEADS, dtype=f32)
    b_gates = jnp.concatenate([nrm(ks[7], (DEPTH, MLSTM_HEADS), 0.1),
                               f_bias + nrm(ks[8], (DEPTH, MLSTM_HEADS), 0.1)], axis=-1)
    conv_qk = nrm(ks[9], (DEPTH, MLSTM_CONV, 2 * MLSTM_QK_WIDTH), MLSTM_CONV ** -0.5)
    mlstm_head_g = 1.0 + nrm(ks[10], (DEPTH, MLSTM_WIDTH), 0.02)
    sinks = nrm(ks[11], (DEPTH, ATTN_HEADS), 0.5)
    w_out = nrm(ks[12], (DEPTH, MLSTM_WIDTH + ATTN_WIDTH, D_MODEL), (MLSTM_WIDTH + ATTN_WIDTH) ** -0.5)
    norm_ffn_g = 1.0 + nrm(ks[13], (DEPTH, D_MODEL), 0.02)
    w_up = nrm(ks[14], (DEPTH, D_MODEL, 2 * D_FF), D_MODEL ** -0.5)
    conv_ffn = nrm(ks[15], (DEPTH, FFN_CONV, D_FF), FFN_CONV ** -0.5)
    w_down = nrm(ks[16], (DEPTH, D_FF, D_MODEL), D_FF ** -0.5)
    final_g = 1.0 + nrm(ks[17], (D_MODEL,), 0.02)
    return {'x': x, 'c': c, 'positions': positions, 'ada_w': ada_w, 'ada_b': ada_b,
            'norm_mix_g': norm_mix_g, 'w_in': w_in, 'b_gates': b_gates, 'conv_qk': conv_qk,
            'mlstm_head_g': mlstm_head_g, 'sinks': sinks, 'w_out': w_out,
            'norm_ffn_g': norm_ffn_g, 'w_up': w_up, 'conv_ffn': conv_ffn, 'w_down': w_down,
            'final_g': final_g}


def reference(x, c, positions, ada_w, ada_b, norm_mix_g, w_in, b_gates, conv_qk, mlstm_head_g,
              sinks, w_out, norm_ffn_g, w_up, conv_ffn, w_down, final_g):
    cos, sin = rope_tables(positions)
    c_act = jax.nn.silu(c)
    for l in range(DEPTH):
        mod = c_act @ ada_w[l] + ada_b[l]
        x = hybrid_layer(x, mod, cos, sin, norm_mix_g[l], w_in[l], b_gates[l], conv_qk[l],
                         mlstm_head_g[l], sinks[l], w_out[l], norm_ffn_g[l], w_up[l],
                         conv_ffn[l], w_down[l])
    return rms_norm(x, final_g)
```

```python
import functools

import jax
import jax.numpy as jnp
from jax import lax
from jax.experimental import pallas as pl
from jax.experimental.pallas import tpu as pltpu

F32 = jnp.float32
BF16 = jnp.bfloat16

MLSTM_HEADS = 4
MLSTM_DQK = 128
MLSTM_DV = 256
MLSTM_CONV = 4
MLSTM_CHUNK = 64
GATE_SOFTCAP = 15.0
ATTN_HEAD_DIM = 64
ATTN_HEADS = 16
ATTN_KV_HEADS = 2
WINDOW = 128
ROPE_DIM = 16
ROPE_THETA = 500000.0
FFN_CONV = 3
NORM_EPS = 1e-6

LANES = 128
SUBLANES = 8
VMEM_LIMIT = 52 * 1024 * 1024

QK_W = 2 * MLSTM_HEADS * MLSTM_DQK
V_W = MLSTM_HEADS * MLSTM_DV
QA_W = ATTN_HEADS * ATTN_HEAD_DIM
KV_W = ATTN_KV_HEADS * ATTN_HEAD_DIM
OFF_QK = 0
OFF_V = OFF_QK + QK_W
OFF_O = OFF_V + V_W
OFF_QA = OFF_O + V_W
OFF_KA = OFF_QA + QA_W
OFF_VA = OFF_KA + KV_W
OFF_G = OFF_VA + KV_W
IN_PACKED = OFF_G + LANES


def _params(*sem):
    return pltpu.CompilerParams(dimension_semantics=sem, vmem_limit_bytes=VMEM_LIMIT)


def _pick_tile(n, pref):
    t = min(n, pref)
    while n % t:
        t //= 2
    return t


def _mod_kernel(c_ref, w_ref, b_ref, o_ref):
    ca = jax.nn.silu(c_ref[...]).astype(BF16)
    o_ref[...] = jnp.dot(ca, w_ref[...].astype(BF16), preferred_element_type=F32) + b_ref[...]


def _modulation(c_pad, ada_w, ada_b):
    depth, d, n = ada_w.shape
    rows = c_pad.shape[0]
    tn = _pick_tile(n, 1024)
    return pl.pallas_call(
        _mod_kernel,
        grid=(depth, n // tn),
        in_specs=[
            pl.BlockSpec((rows, d), lambda l, j: (0, 0)),
            pl.BlockSpec((None, d, tn), lambda l, j: (l, 0, j)),
            pl.BlockSpec((None, 1, tn), lambda l, j: (l, 0, j)),
        ],
        out_specs=pl.BlockSpec((None, rows, tn), lambda l, j: (l, 0, j)),
        out_shape=jax.ShapeDtypeStruct((depth, rows, n), F32),
        compiler_params=_params("arbitrary", "arbitrary"),
        name="adaln_mod",
    )(c_pad, ada_w, ada_b.reshape(depth, 1, n))


def _rope_kernel(pos_ref, freq_ref, cos_ref, sina_ref, sinb_ref):
    ang = pos_ref[...].astype(F32) * freq_ref[...]
    d = lax.broadcasted_iota(jnp.int32, ang.shape, 1) % ATTN_HEAD_DIM
    c = jnp.cos(ang)
    s = jnp.sin(ang)
    half = ROPE_DIM // 2
    cos_ref[...] = jnp.where(d < ROPE_DIM, c, 1.0)
    sina_ref[...] = jnp.where(d < half, -s, 0.0)
    sinb_ref[...] = jnp.where((d >= half) & (d < ROPE_DIM), s, 0.0)


def _rope_tables(positions):
    b, s = positions.shape
    ts = _pick_tile(s, 1024)
    inv_freq = ROPE_THETA ** (-jnp.arange(0, ROPE_DIM, 2, dtype=F32) / ROPE_DIM)
    per_head = jnp.concatenate(
        [inv_freq, inv_freq, jnp.zeros((ATTN_HEAD_DIM - ROPE_DIM,), F32)])
    freq = jnp.tile(per_head, LANES // ATTN_HEAD_DIM).reshape(1, LANES)
    tab = jax.ShapeDtypeStruct((b, s, LANES), F32)
    spec = pl.BlockSpec((None, ts, LANES), lambda i, j: (i, j, 0))
    return pl.pallas_call(
        _rope_kernel,
        grid=(b, s // ts),
        in_specs=[pl.BlockSpec((None, ts, 1), lambda i, j: (i, j, 0)),
                  pl.BlockSpec((1, LANES), lambda i, j: (0, 0))],
        out_specs=[spec, spec, spec],
        out_shape=[tab, tab, tab],
        compiler_params=_params("arbitrary", "arbitrary"),
        name="rope_tables",
    )(positions.reshape(b, s, 1), freq)


def _adaln(x, g, shift, scale):
    ms = jnp.mean(x * x, axis=-1, keepdims=True)
    h = x * lax.rsqrt(ms + NORM_EPS) * g
    return h * (1.0 + scale) + shift


def _rope(t, cos, sina, sinb):
    n = t.shape[-1]
    half = ROPE_DIM // 2
    return t * cos + pltpu.roll(t, n - half, 1) * sina + pltpu.roll(t, half, 1) * sinb


def _inproj_kernel(x_ref, shift_ref, scale_ref, g_ref, w_ref, cw_ref, bg_ref,
                   cos_ref, sina_ref, sinb_ref,
                   qk_ref, v_ref, o_ref, gates_ref, qa_ref, ka_ref, va_ref,
                   hb_ref, cbuf_ref):
    tm = x_ref.shape[0]
    halo = SUBLANES
    hb_ref[...] = _adaln(x_ref[...], g_ref[...], shift_ref[...], scale_ref[...]).astype(BF16)

    def proj(off, width):
        return jnp.dot(hb_ref[...], w_ref[:, off:off + width], preferred_element_type=F32)

    @pl.when(pl.program_id(1) == 0)
    def _():
        cbuf_ref[0:halo, :] = jnp.zeros((halo, QK_W), F32)

    cw = cw_ref[...]
    half = QK_W // 2
    for c in range(2):
        lo = c * half
        pq = proj(OFF_QK + lo, half)
        cbuf_ref[halo:halo + tm, lo:lo + half] = pq
        acc = cbuf_ref[halo - 3:halo - 3 + tm, lo:lo + half] * cw[0:1, lo:lo + half]
        acc = acc + cbuf_ref[halo - 2:halo - 2 + tm, lo:lo + half] * cw[1:2, lo:lo + half]
        acc = acc + cbuf_ref[halo - 1:halo - 1 + tm, lo:lo + half] * cw[2:3, lo:lo + half]
        acc = acc + pq * cw[3:4, lo:lo + half]
        act = jax.nn.silu(acc)
        if c == 0:
            act = act * (MLSTM_DQK ** -0.5)
        qk_ref[:, lo:lo + half] = act.astype(BF16)
    cbuf_ref[0:halo, :] = cbuf_ref[tm:tm + halo, :]

    for c in range(2):
        lo = c * (V_W // 2)
        v_ref[:, lo:lo + V_W // 2] = proj(OFF_V + lo, V_W // 2).astype(BF16)
    for c in range(2):
        lo = c * (V_W // 2)
        o_ref[:, lo:lo + V_W // 2] = jax.nn.sigmoid(proj(OFF_O + lo, V_W // 2))

    cos = cos_ref[...]
    sina = sina_ref[...]
    sinb = sinb_ref[...]
    for c in range(QA_W // 256):
        pa = proj(OFF_QA + c * 256, 256)
        for k in range(2):
            lo = c * 256 + k * LANES
            r = _rope(pa[:, k * LANES:(k + 1) * LANES], cos, sina, sinb)
            qa_ref[:, lo:lo + LANES] = (r * (ATTN_HEAD_DIM ** -0.5)).astype(BF16)

    pkv = proj(OFF_KA, 2 * KV_W)
    ka_ref[...] = _rope(pkv[:, 0:KV_W], cos, sina, sinb)
    va_ref[...] = pkv[:, KV_W:2 * KV_W]

    pg = proj(OFF_G, LANES) + bg_ref[...]
    gates_ref[...] = GATE_SOFTCAP * jnp.tanh(pg / GATE_SOFTCAP)


def _inproj(x, shift, scale, g, w_packed, conv_qk, bg, cos, sina, sinb):
    b, s, d = x.shape
    tm = _pick_tile(s, 512)
    row = lambda i, j: (i, j, 0)
    vec = lambda i, j: (i, 0, 0)
    const = lambda i, j: (0, 0)
    outs = [
        (QK_W, BF16), (V_W, BF16), (V_W, F32), (LANES, F32), (QA_W, BF16), (KV_W, F32), (KV_W, F32),
    ]
    return pl.pallas_call(
        _inproj_kernel,
        grid=(b, s // tm),
        in_specs=[
            pl.BlockSpec((None, tm, d), row),
            pl.BlockSpec((None, 1, d), vec),
            pl.BlockSpec((None, 1, d), vec),
            pl.BlockSpec((1, d), const),
            pl.BlockSpec((d, IN_PACKED), const),
            pl.BlockSpec((MLSTM_CONV, QK_W), const),
            pl.BlockSpec((1, LANES), const),
            pl.BlockSpec((None, tm, LANES), row),
            pl.BlockSpec((None, tm, LANES), row),
            pl.BlockSpec((None, tm, LANES), row),
        ],
        out_specs=[pl.BlockSpec((None, tm, w), row) for w, _ in outs],
        out_shape=[jax.ShapeDtypeStruct((b, s, w), dt) for w, dt in outs],
        scratch_shapes=[pltpu.VMEM((tm, d), BF16), pltpu.VMEM((tm + SUBLANES, QK_W), F32)],
        compiler_params=_params("arbitrary", "arbitrary"),
        name="in_proj",
    )(x, shift, scale, g, w_packed, conv_qk, bg, cos, sina, sinb)


def _seg_scan(x, op, fill):
    pos = lax.broadcasted_iota(jnp.int32, x.shape, 1) % MLSTM_CHUNK
    step = 1
    while step < MLSTM_CHUNK:
        shifted = jnp.where(pos >= step, pltpu.roll(x, step, 1), fill)
        x = op(x, shifted)
        step *= 2
    return x


def _gateprep_kernel(gt_ref, a_ref, cg_ref):
    gt = gt_ref[...]
    lf = jax.nn.log_sigmoid(gt)
    b = _seg_scan(lf, jnp.add, 0.0)
    g = gt - pltpu.roll(b, MLSTM_HEADS, 0)
    row = lax.broadcasted_iota(jnp.int32, gt.shape, 0)
    a_ref[...] = jnp.where(row < MLSTM_HEADS, g, b)
    cg_ref[...] = _seg_scan(g, jnp.maximum, -jnp.inf)


def _gateprep(gates_t):
    b, r, s = gates_t.shape
    spec = pl.BlockSpec((None, r, s), lambda i: (i, 0, 0))
    shp = jax.ShapeDtypeStruct((b, r, s), F32)
    return pl.pallas_call(
        _gateprep_kernel,
        grid=(b,),
        in_specs=[spec],
        out_specs=[spec, spec],
        out_shape=[shp, shp],
        compiler_params=_params("arbitrary"),
        name="gate_prep",
    )(gates_t)


def _mlstm_kernel(q_ref, k_ref, v_ref, o_ref, grow_ref, col_ref, hg_ref, y_ref, c_ref, m_ref):
    L = MLSTM_CHUNK
    n_chunks = q_ref.shape[0] // L

    @pl.when(pl.program_id(2) == 0)
    def _():
        c_ref[...] = jnp.zeros(c_ref.shape, F32)
        m_ref[...] = jnp.zeros(m_ref.shape, F32)

    ri = lax.broadcasted_iota(jnp.int32, (L, L), 0)
    ci = lax.broadcasted_iota(jnp.int32, (L, L), 1)
    causal = ci <= ri
    ones_col = (lax.broadcasted_iota(jnp.int32, (L, LANES), 1) == 0).astype(BF16)
    hg = hg_ref[...]
    m = m_ref[0:1, 0:1]

    for c in range(n_chunks):
        rows = slice(c * L, (c + 1) * L)
        q = q_ref[rows, :]
        k = k_ref[rows, :]
        v_ext = jnp.concatenate([v_ref[rows, :], ones_col], axis=1)
        col = col_ref[rows, :]
        g_col, b_col, cg_col = col[:, 0:1], col[:, 1:2], col[:, 2:3]
        g_row = grow_ref[c:c + 1, :]

        mm_col = jnp.maximum(m, cg_col)
        mm_last = mm_col[L - 1:L, :]
        w_intra = jnp.exp(jnp.where(causal, g_row - mm_col, -jnp.inf))
        w_inter = jnp.exp(m - mm_col)

        s = lax.dot_general(q, k, (((1,), (1,)), ((), ())), preferred_element_type=F32) * w_intra
        inter = jnp.dot(q, c_ref[...].astype(BF16), preferred_element_type=F32)
        intra = jnp.dot(s.astype(BF16), v_ext[:, 0:MLSTM_DV], preferred_element_type=F32)
        num = intra + w_inter * inter[:, 0:MLSTM_DV]
        nq = jnp.sum(s, axis=-1, keepdims=True) + w_inter * inter[:, MLSTM_DV:MLSTM_DV + 1]
        den = jnp.maximum(jnp.abs(nq), jnp.exp(-(b_col + mm_col)))
        h = num / den

        w_state = jnp.exp(g_col - mm_last)
        decay = jnp.exp(m - mm_last)
        kw = (k.astype(F32) * w_state).astype(BF16)
        upd = lax.dot_general(kw, v_ext, (((0,), (0,)), ((), ())), preferred_element_type=F32)
        c_ref[...] = decay * c_ref[...] + upd
        m = b_col[L - 1:L, :] + mm_last

        hn = h * lax.rsqrt(jnp.mean(h * h, axis=-1, keepdims=True) + NORM_EPS) * hg
        y_ref[rows, :] = (o_ref[rows, :] * hn).astype(BF16)

    m_ref[...] = jnp.broadcast_to(m, m_ref.shape)


def _mlstm(qk, v, o_sig, grow, cols, head_g):
    b, s, _ = qk.shape
    t = _pick_tile(s, 512)
    nh = MLSTM_HEADS
    return pl.pallas_call(
        _mlstm_kernel,
        grid=(b, nh, s // t),
        in_specs=[
            pl.BlockSpec((None, t, MLSTM_DQK), lambda i, h, j: (i, j, h)),
            pl.BlockSpec((None, t, MLSTM_DQK), lambda i, h, j: (i, j, nh + h)),
            pl.BlockSpec((None, t, MLSTM_DV), lambda i, h, j: (i, j, h)),
            pl.BlockSpec((None, t, MLSTM_DV), lambda i, h, j: (i, j, h)),
            pl.BlockSpec((None, None, t // MLSTM_CHUNK, MLSTM_CHUNK), lambda i, h, j: (i, h, j, 0)),
            pl.BlockSpec((None, None, t, 4), lambda i, h, j: (i, h, j, 0)),
            pl.BlockSpec((1, MLSTM_DV), lambda i, h, j: (0, h)),
        ],
        out_specs=pl.BlockSpec((None, t, MLSTM_DV), lambda i, h, j: (i, j, h)),
        out_shape=jax.ShapeDtypeStruct((b, s, V_W), BF16),
        scratch_shapes=[pltpu.VMEM((MLSTM_DQK, MLSTM_DV + LANES), F32),
                        pltpu.VMEM((SUBLANES, LANES), F32)],
        compiler_params=_params("arbitrary", "arbitrary", "arbitrary"),
        name="mlstm",
    )(qk, qk, v, o_sig, grow, cols, head_g)


def _block_diag(band, kv_head):
    lane = lax.broadcasted_iota(jnp.int32, band.shape, 1)
    low = lane < ATTN_HEAD_DIM
    rolled = pltpu.roll(band, ATTN_HEAD_DIM, 1)
    if kv_head == 0:
        top = jnp.where(low, band, 0.0)
        bot = jnp.where(low, 0.0, rolled)
    else:
        top = jnp.where(low, rolled, 0.0)
        bot = jnp.where(low, 0.0, band)
    return jnp.concatenate([top, bot], axis=0).astype(BF16)


def _attn_kernel(q_ref, kc_ref, kp_ref, vc_ref, vp_ref, sink_ref, y_ref):
    W = WINDOW
    n_blocks = q_ref.shape[0] // W
    pairs = ATTN_HEADS // ATTN_KV_HEADS // 2
    first_key = jnp.where(pl.program_id(1) == 0, W, 0)

    qi = lax.broadcasted_iota(jnp.int32, (W, 4 * W), 0)
    ki = lax.broadcasted_iota(jnp.int32, (W, 4 * W), 1) % (2 * W)
    in_band = (ki > qi) & (ki <= qi + W)
    even = lax.broadcasted_iota(jnp.int32, (1, 4 * W), 1) < 2 * W
    out_even = lax.broadcasted_iota(jnp.int32, (1, LANES), 1) < ATTN_HEAD_DIM
    sinks = sink_ref[...]

    for blk in range(n_blocks):
        rows = slice(blk * W, (blk + 1) * W)
        if blk == 0:
            k_prev, v_prev = kp_ref[...], vp_ref[...]
            mask = in_band & (ki >= first_key)
        else:
            prev = slice((blk - 1) * W, blk * W)
            k_prev, v_prev = kc_ref[prev, :], vc_ref[prev, :]
            mask = in_band
        k_band = jnp.concatenate([k_prev, kc_ref[rows, :]], axis=0)
        v_band = jnp.concatenate([v_prev, vc_ref[rows, :]], axis=0)
        for kvh in range(ATTN_KV_HEADS):
            k_bd = _block_diag(k_band, kvh)
            v_bd = _block_diag(v_band, kvh)
            for p in range(pairs):
                head = kvh * 2 * pairs + 2 * p
                lanes = slice(head * ATTN_HEAD_DIM, (head + 2) * ATTN_HEAD_DIM)
                sc = lax.dot_general(q_ref[rows, lanes], k_bd, (((1,), (1,)), ((), ())),
                                     preferred_element_type=F32)
                sc = jnp.where(mask, sc, -jnp.inf)
                sink_e = sinks[0:1, head:head + 1]
                sink_o = sinks[0:1, head + 1:head + 2]
                m_e = jnp.maximum(jnp.max(sc[:, :2 * W], axis=-1, keepdims=True), sink_e)
                m_o = jnp.maximum(jnp.max(sc[:, 2 * W:], axis=-1, keepdims=True), sink_o)
                pr = jnp.exp(sc - jnp.where(even, m_e, m_o))
                d_e = jnp.sum(pr[:, :2 * W], axis=-1, keepdims=True) + jnp.exp(sink_e - m_e)
                d_o = jnp.sum(pr[:, 2 * W:], axis=-1, keepdims=True) + jnp.exp(sink_o - m_o)
                out = jnp.dot(pr.astype(BF16), v_bd, preferred_element_type=F32)
                y_ref[rows, lanes] = (out / jnp.where(out_even, d_e, d_o)).astype(BF16)


def _attention(qa, ka, va, sinks):
    b, s, _ = qa.shape
    tq = _pick_tile(s, 512)
    per = tq // WINDOW
    cur = lambda i, j: (i, j, 0)
    prev = lambda i, j: (i, jnp.maximum(j * per - 1, 0), 0)
    return pl.pallas_call(
        _attn_kernel,
        grid=(b, s // tq),
        in_specs=[
            pl.BlockSpec((None, tq, QA_W), cur),
            pl.BlockSpec((None, tq, KV_W), cur),
            pl.BlockSpec((None, WINDOW, KV_W), prev),
            pl.BlockSpec((None, tq, KV_W), cur),
            pl.BlockSpec((None, WINDOW, KV_W), prev),
            pl.BlockSpec((1, ATTN_HEADS), lambda i, j: (0, 0)),
        ],
        out_specs=pl.BlockSpec((None, tq, QA_W), cur),
        out_shape=jax.ShapeDtypeStruct((b, s, QA_W), BF16),
        compiler_params=_params("arbitrary", "arbitrary"),
        name="swa_attention",
    )(qa, ka, ka, va, va, sinks)


def _outproj_kernel(ym_ref, ya_ref, x_ref, gate_ref, w_ref, o_ref):
    km = ym_ref.shape[1]
    y = jnp.dot(ym_ref[...], w_ref[0:km, :], preferred_element_type=F32)
    y = y + jnp.dot(ya_ref[...], w_ref[km:, :], preferred_element_type=F32)
    o_ref[...] = x_ref[...] + gate_ref[...] * y


def _outproj(ym, ya, x, gate, w_out):
    b, s, d = x.shape
    tm = _pick_tile(s, 512)
    row = lambda i, j: (i, j, 0)
    return pl.pallas_call(
        _outproj_kernel,
        grid=(b, s // tm),
        in_specs=[
            pl.BlockSpec((None, tm, ym.shape[2]), row),
            pl.BlockSpec((None, tm, ya.shape[2]), row),
            pl.BlockSpec((None, tm, d), row),
            pl.BlockSpec((None, 1, d), lambda i, j: (i, 0, 0)),
            pl.BlockSpec(w_out.shape, lambda i, j: (0, 0)),
        ],
        out_specs=pl.BlockSpec((None, tm, d), row),
        out_shape=jax.ShapeDtypeStruct((b, s, d), F32),
        compiler_params=_params("arbitrary", "arbitrary"),
        name="out_proj",
    )(ym, ya, x, gate, w_out)


def _ffn_kernel(x_ref, shift_ref, scale_ref, gate_ref, g_ref, fg_ref, wg_ref, wu_ref, cw_ref, wd_ref,
                o_ref, hb_ref, gbuf_ref, tail_ref, *, final_norm):
    tm = x_ref.shape[0]
    halo = SUBLANES
    i = pl.program_id(1)
    j = pl.program_id(2)

    @pl.when(j == 0)
    def _():
        hb_ref[...] = _adaln(x_ref[...], g_ref[...], shift_ref[...], scale_ref[...]).astype(BF16)
        o_ref[...] = jnp.zeros(o_ref.shape, F32)

    g = jnp.dot(hb_ref[...], wg_ref[...], preferred_element_type=F32)
    u = jnp.dot(hb_ref[...], wu_ref[...], preferred_element_type=F32)

    @pl.when(i == 0)
    def _():
        gbuf_ref[0:halo, :] = jnp.zeros((halo, g.shape[1]), F32)

    @pl.when(i > 0)
    def _():
        gbuf_ref[0:halo, :] = tail_ref[j]

    gbuf_ref[halo:halo + tm, :] = g
    tail_ref[j] = g[tm - halo:tm, :]
    cw = cw_ref[...]
    conv = gbuf_ref[halo - 2:halo - 2 + tm, :] * cw[0:1, :]
    conv = conv + gbuf_ref[halo - 1:halo - 1 + tm, :] * cw[1:2, :]
    conv = conv + g * cw[2:3, :]
    act = (jax.nn.silu(conv) * u).astype(BF16)
    o_ref[...] += jnp.dot(act, wd_ref[...], preferred_element_type=F32)

    @pl.when(j == pl.num_programs(2) - 1)
    def _():
        res = x_ref[...] + gate_ref[...] * o_ref[...]
        if final_norm:
            ms = jnp.mean(res * res, axis=-1, keepdims=True)
            res = res * lax.rsqrt(ms + NORM_EPS) * fg_ref[...]
        o_ref[...] = res


def _ffn(x, shift, scale, gate, g, final_g, w_up, conv_ffn, w_down, final_norm):
    b, s, d = x.shape
    ff = w_down.shape[0]
    tm = _pick_tile(s, 512)
    tf = _pick_tile(ff, 512)
    nf = ff // tf
    row = lambda i, t, j: (i, t, 0)
    vec = lambda i, t, j: (i, 0, 0)
    const = lambda i, t, j: (0, 0)
    return pl.pallas_call(
        functools.partial(_ffn_kernel, final_norm=final_norm),
        grid=(b, s // tm, nf),
        in_specs=[
            pl.BlockSpec((None, tm, d), row),
            pl.BlockSpec((None, 1, d), vec),
            pl.BlockSpec((None, 1, d), vec),
            pl.BlockSpec((None, 1, d), vec),
            pl.BlockSpec((1, d), const),
            pl.BlockSpec((1, d), const),
            pl.BlockSpec((d, tf), lambda i, t, j: (0, j)),
            pl.BlockSpec((d, tf), lambda i, t, j: (0, nf + j)),
            pl.BlockSpec((FFN_CONV, tf), lambda i, t, j: (0, j)),
            pl.BlockSpec((tf, d), lambda i, t, j: (j, 0)),
        ],
        out_specs=pl.BlockSpec((None, tm, d), row),
        out_shape=jax.ShapeDtypeStruct((b, s, d), F32),
        scratch_shapes=[pltpu.VMEM((tm, d), BF16),
                        pltpu.VMEM((tm + SUBLANES, tf), F32),
                        pltpu.VMEM((nf, SUBLANES, tf), F32)],
        compiler_params=_params("arbitrary", "arbitrary", "arbitrary"),
        name="convglu_ffn",
    )(x, shift, scale, gate, g, final_g, w_up, w_up, conv_ffn, w_down)


def _pack_w_in(w_in):
    nh = MLSTM_HEADS
    o_gates = QK_W + 2 * V_W
    o_attn = o_gates + 2 * nh
    gates = w_in[:, :, o_gates:o_attn]
    gates = jnp.pad(gates, ((0, 0), (0, 0), (0, LANES - 2 * nh)))
    packed = jnp.concatenate([w_in[:, :, :o_gates], w_in[:, :, o_attn:], gates], axis=-1)
    return packed.astype(BF16)


def kernel(x, c, positions, ada_w, ada_b, norm_mix_g, w_in, b_gates, conv_qk, mlstm_head_g, sinks,
           w_out, norm_ffn_g, w_up, conv_ffn, w_down, final_g):
    b, s, d = x.shape
    depth = ada_w.shape[0]
    nh = MLSTM_HEADS

    c_pad = jnp.pad(c, ((0, 2 * SUBLANES - b), (0, 0)))
    mod = _modulation(c_pad, ada_w, ada_b)[:, :b].reshape(depth, b, 6, 1, d)
    cos, sina, sinb = _rope_tables(positions)

    w_in_p = _pack_w_in(w_in)
    w_out_b = w_out.astype(BF16)
    w_up_b = w_up.astype(BF16)
    w_down_b = w_down.astype(BF16)
    bg = jnp.pad(b_gates, ((0, 0), (0, LANES - 2 * nh))).reshape(depth, 1, LANES)
    final_g2 = final_g.reshape(1, d)

    for l in range(depth):
        shift_m, scale_m, gate_m, shift_f, scale_f, gate_f = (mod[l, :, k] for k in range(6))
        qk, v, o_sig, gates, qa, ka, va = _inproj(
            x, shift_m, scale_m, norm_mix_g[l].reshape(1, d), w_in_p[l], conv_qk[l], bg[l],
            cos, sina, sinb)

        gates_t = jnp.transpose(gates[:, :, :2 * nh], (0, 2, 1))
        a, cg = _gateprep(gates_t)
        grow = a[:, :nh].reshape(b, nh, s // MLSTM_CHUNK, MLSTM_CHUNK)
        cols = jnp.stack([a[:, :nh], a[:, nh:], cg[:, :nh], cg[:, :nh]], axis=-1)
        ym = _mlstm(qk, v, o_sig, grow, cols, mlstm_head_g[l].reshape(1, V_W))

        ya = _attention(qa, ka, va, sinks[l].reshape(1, ATTN_HEADS))
        x = _outproj(ym, ya, x, gate_m, w_out_b[l])
        x = _ffn(x, shift_f, scale_f, gate_f, norm_ffn_g[l].reshape(1, d), final_g2,
                 w_up_b[l], conv_ffn[l], w_down_b[l], final_norm=(l == depth - 1))
    return x
```

```python
import functools

import jax
import jax.numpy as jnp
from jax import lax
from jax.experimental import pallas as pl
from jax.experimental.pallas import tpu as pltpu

F32 = jnp.float32
BF16 = jnp.bfloat16

MLSTM_HEADS = 4
MLSTM_DQK = 128
MLSTM_DV = 256
MLSTM_CONV = 4
MLSTM_CHUNK = 64
GATE_SOFTCAP = 15.0
ATTN_HEAD_DIM = 64
ATTN_HEADS = 16
ATTN_KV_HEADS = 2
WINDOW = 128
ROPE_DIM = 16
ROPE_THETA = 500000.0
FFN_CONV = 3
NORM_EPS = 1e-6

LANES = 128
SUBLANES = 8
VMEM_LIMIT = 52 * 1024 * 1024

QK_W = 2 * MLSTM_HEADS * MLSTM_DQK
V_W = MLSTM_HEADS * MLSTM_DV
QA_W = ATTN_HEADS * ATTN_HEAD_DIM
KV_W = ATTN_KV_HEADS * ATTN_HEAD_DIM
OFF_QK = 0
OFF_V = OFF_QK + QK_W
OFF_O = OFF_V + V_W
OFF_QA = OFF_O + V_W
OFF_KA = OFF_QA + QA_W
OFF_VA = OFF_KA + KV_W
OFF_G = OFF_VA + KV_W
IN_PACKED = OFF_G + LANES

COL_G = 0
COL_B = MLSTM_HEADS
COL_CG = 2 * MLSTM_HEADS
PAIRS = ATTN_HEADS // ATTN_KV_HEADS // 2


def _params(*sem):
    return pltpu.CompilerParams(dimension_semantics=sem, vmem_limit_bytes=VMEM_LIMIT)


def _pick_tile(n, pref):
    t = min(n, pref)
    while n % t:
        t //= 2
    return t


def _mod_kernel(c_ref, w_ref, b_ref, o_ref):
    ca = jax.nn.silu(c_ref[...]).astype(BF16)
    o_ref[...] = jnp.dot(ca, w_ref[...].astype(BF16), preferred_element_type=F32) + b_ref[...]


def _modulation(c_pad, ada_w, ada_b):
    depth, d, n = ada_w.shape
    rows = c_pad.shape[0]
    tn = _pick_tile(n, 1024)
    return pl.pallas_call(
        _mod_kernel,
        grid=(depth, n // tn),
        in_specs=[
            pl.BlockSpec((rows, d), lambda l, j: (0, 0)),
            pl.BlockSpec((None, d, tn), lambda l, j: (l, 0, j)),
            pl.BlockSpec((None, 1, tn), lambda l, j: (l, 0, j)),
        ],
        out_specs=pl.BlockSpec((None, rows, tn), lambda l, j: (l, 0, j)),
        out_shape=jax.ShapeDtypeStruct((depth, rows, n), F32),
        compiler_params=_params("arbitrary", "arbitrary"),
        name="adaln_mod",
    )(c_pad, ada_w, ada_b.reshape(depth, 1, n))


def _rope_kernel(pos_ref, freq_ref, cos_ref, sina_ref, sinb_ref):
    ang = pos_ref[...].astype(F32) * freq_ref[...]
    d = lax.broadcasted_iota(jnp.int32, ang.shape, 1) % ATTN_HEAD_DIM
    c = jnp.cos(ang)
    s = jnp.sin(ang)
    half = ROPE_DIM // 2
    cos_ref[...] = jnp.where(d < ROPE_DIM, c, 1.0)
    sina_ref[...] = jnp.where(d < half, -s, 0.0)
    sinb_ref[...] = jnp.where((d >= half) & (d < ROPE_DIM), s, 0.0)


def _rope_tables(positions):
    b, s = positions.shape
    ts = _pick_tile(s, 1024)
    inv_freq = ROPE_THETA ** (-jnp.arange(0, ROPE_DIM, 2, dtype=F32) / ROPE_DIM)
    per_head = jnp.concatenate(
        [inv_freq, inv_freq, jnp.zeros((ATTN_HEAD_DIM - ROPE_DIM,), F32)])
    freq = jnp.tile(per_head, LANES // ATTN_HEAD_DIM).reshape(1, LANES)
    tab = jax.ShapeDtypeStruct((b, s, LANES), F32)
    spec = pl.BlockSpec((None, ts, LANES), lambda i, j: (i, j, 0))
    return pl.pallas_call(
        _rope_kernel,
        grid=(b, s // ts),
        in_specs=[pl.BlockSpec((None, ts, 1), lambda i, j: (i, j, 0)),
                  pl.BlockSpec((1, LANES), lambda i, j: (0, 0))],
        out_specs=[spec, spec, spec],
        out_shape=[tab, tab, tab],
        compiler_params=_params("arbitrary", "arbitrary"),
        name="rope_tables",
    )(positions.reshape(b, s, 1), freq)


def _adaln(x, g, shift, scale):
    ms = jnp.mean(x * x, axis=-1, keepdims=True)
    h = x * lax.rsqrt(ms + NORM_EPS) * g
    return h * (1.0 + scale) + shift


def _rope(t, cos, sina, sinb):
    n = t.shape[-1]
    half = ROPE_DIM // 2
    return t * cos + pltpu.roll(t, n - half, 1) * sina + pltpu.roll(t, half, 1) * sinb


def _chunk_scan_rows(x, op, fill):
    pos = lax.broadcasted_iota(jnp.int32, x.shape, 0) % MLSTM_CHUNK
    step = 1
    while step < MLSTM_CHUNK:
        x = op(x, jnp.where(pos >= step, pltpu.roll(x, step, 0), fill))
        step *= 2
    return x


def _gate_columns(gates):
    nh = MLSTM_HEADS
    lane = lax.broadcasted_iota(jnp.int32, gates.shape, 1)
    b = _chunk_scan_rows(jax.nn.log_sigmoid(gates), jnp.add, 0.0)
    g = gates - pltpu.roll(b, LANES - nh, 1)
    cg = _chunk_scan_rows(g, jnp.maximum, -jnp.inf)
    return jnp.where(lane < COL_B, g, jnp.where(lane < COL_CG, b, pltpu.roll(cg, COL_CG, 1)))


def _inproj_kernel(x_ref, shift_ref, scale_ref, g_ref, w_ref, cw_ref, bg_ref,
                   cos_ref, sina_ref, sinb_ref,
                   qk_ref, v_ref, o_ref, col_ref, qa_ref, ka_ref, va_ref,
                   hb_ref, cbuf_ref):
    tm = x_ref.shape[0]
    halo = SUBLANES

    @pl.when(pl.program_id(1) == 0)
    def _():
        cbuf_ref[0:halo, :] = jnp.zeros((halo, QK_W), F32)

    hb_ref[...] = _adaln(x_ref[...], g_ref[...], shift_ref[...], scale_ref[...]).astype(BF16)

    def proj(off, width):
        return jnp.dot(hb_ref[...], w_ref[:, off:off + width], preferred_element_type=F32)

    cw = cw_ref[...]
    half = QK_W // 2
    for c in range(2):
        lo = c * half
        pq = proj(OFF_QK + lo, half)
        cbuf_ref[halo:halo + tm, lo:lo + half] = pq
        acc = cbuf_ref[halo - 3:halo - 3 + tm, lo:lo + half] * cw[0:1, lo:lo + half]
        acc = acc + cbuf_ref[halo - 2:halo - 2 + tm, lo:lo + half] * cw[1:2, lo:lo + half]
        acc = acc + cbuf_ref[halo - 1:halo - 1 + tm, lo:lo + half] * cw[2:3, lo:lo + half]
        acc = acc + pq * cw[3:4, lo:lo + half]
        act = jax.nn.silu(acc)
        if c == 0:
            act = act * (MLSTM_DQK ** -0.5)
        qk_ref[:, lo:lo + half] = act.astype(BF16)
    cbuf_ref[0:halo, :] = cbuf_ref[tm:tm + halo, :]

    for c in range(2):
        lo = c * (V_W // 2)
        v_ref[:, lo:lo + V_W // 2] = proj(OFF_V + lo, V_W // 2).astype(BF16)
    for c in range(2):
        lo = c * (V_W // 2)
        o_ref[:, lo:lo + V_W // 2] = jax.nn.sigmoid(proj(OFF_O + lo, V_W // 2))

    cos = cos_ref[...]
    sina = sina_ref[...]
    sinb = sinb_ref[...]
    for c in range(QA_W // 256):
        pa = proj(OFF_QA + c * 256, 256)
        for k in range(2):
            lo = c * 256 + k * LANES
            r = _rope(pa[:, k * LANES:(k + 1) * LANES], cos, sina, sinb)
            qa_ref[:, lo:lo + LANES] = (r * (ATTN_HEAD_DIM ** -0.5)).astype(BF16)

    pkv = proj(OFF_KA, 2 * KV_W)
    ka_ref[...] = _rope(pkv[:, 0:KV_W], cos, sina, sinb)
    va_ref[...] = pkv[:, KV_W:2 * KV_W]

    pg = proj(OFF_G, LANES) + bg_ref[...]
    col_ref[...] = _gate_columns(GATE_SOFTCAP * jnp.tanh(pg / GATE_SOFTCAP))


def _inproj(l, x, mod, g, w_packed, conv_qk, bg, cos, sina, sinb):
    b, s, d = x.shape
    tm = _pick_tile(s, 512)
    row = lambda i, j: (i, j, 0)
    lay = lambda i, j: (l, 0, 0)
    outs = [
        (QK_W, BF16), (V_W, BF16), (V_W, F32), (LANES, F32), (QA_W, BF16), (KV_W, F32), (KV_W, F32),
    ]
    return pl.pallas_call(
        _inproj_kernel,
        grid=(b, s // tm),
        in_specs=[
            pl.BlockSpec((None, tm, d), row),
            pl.BlockSpec((None, None, None, 1, d), lambda i, j: (l, i, 0, 0, 0)),
            pl.BlockSpec((None, None, None, 1, d), lambda i, j: (l, i, 1, 0, 0)),
            pl.BlockSpec((None, 1, d), lay),
            pl.BlockSpec((None, d, IN_PACKED), lay),
            pl.BlockSpec((None, MLSTM_CONV, QK_W), lay),
            pl.BlockSpec((None, 1, LANES), lay),
            pl.BlockSpec((None, tm, LANES), row),
            pl.BlockSpec((None, tm, LANES), row),
            pl.BlockSpec((None, tm, LANES), row),
        ],
        out_specs=[pl.BlockSpec((None, tm, w), row) for w, _ in outs],
        out_shape=[jax.ShapeDtypeStruct((b, s, w), dt) for w, dt in outs],
        scratch_shapes=[pltpu.VMEM((tm, d), BF16), pltpu.VMEM((tm + SUBLANES, QK_W), F32)],
        compiler_params=_params("arbitrary", "arbitrary"),
        name="in_proj",
    )(x, mod, mod, g, w_packed, conv_qk, bg, cos, sina, sinb)


def _mlstm_kernel(qk_ref, v_ref, o_ref, col_ref, hg_ref, y_ref, c_ref, m_ref):
    L = MLSTM_CHUNK
    nh = MLSTM_HEADS
    n_chunks = qk_ref.shape[0] // L

    @pl.when(pl.program_id(1) == 0)
    def _():
        c_ref[...] = jnp.zeros(c_ref.shape, F32)
        m_ref[...] = jnp.zeros(m_ref.shape, F32)

    ri = lax.broadcasted_iota(jnp.int32, (L, L), 0)
    ci = lax.broadcasted_iota(jnp.int32, (L, L), 1)
    causal = ci <= ri
    ones_col = (lax.broadcasted_iota(jnp.int32, (L, LANES), 1) == 0).astype(BF16)
    nt = (((1,), (1,)), ((), ()))
    tn = (((0,), (0,)), ((), ()))
    units = [(c, h) for c in range(n_chunks) for h in range(nh)]
    rows = [slice(c * L, (c + 1) * L) for c in range(n_chunks)]
    dvs = [slice(h * MLSTM_DV, (h + 1) * MLSTM_DV) for h in range(nh)]

    col = [col_ref[r, :] for r in rows]
    col_t = [cl.T for cl in col]
    q = {(c, h): qk_ref[rows[c], h * MLSTM_DQK:(h + 1) * MLSTM_DQK] for c, h in units}
    k = {(c, h): qk_ref[rows[c], (nh + h) * MLSTM_DQK:(nh + h + 1) * MLSTM_DQK] for c, h in units}
    v_ext = {(c, h): jnp.concatenate([v_ref[rows[c], dvs[h]], ones_col], axis=1) for c, h in units}

    mm_col, w_inter, decay, w_state = {}, {}, {}, {}
    for h in range(nh):
        m = m_ref[h:h + 1, 0:1]
        for c in range(n_chunks):
            mm = jnp.maximum(m, col[c][:, COL_CG + h:COL_CG + h + 1])
            mm_last = mm[L - 1:L, :]
            mm_col[c, h] = mm
            w_inter[c, h] = jnp.exp(m - mm)
            decay[c, h] = jnp.exp(m - mm_last)
            w_state[c, h] = jnp.exp(col[c][:, COL_G + h:COL_G + h + 1] - mm_last)
            m = col[c][L - 1:L, COL_B + h:COL_B + h + 1] + mm_last
        m_ref[h:h + 1, :] = jnp.broadcast_to(m, (1, LANES))

    s = {}
    for u in units:
        c, h = u
        g_row = col_t[c][COL_G + h:COL_G + h + 1, :]
        w_intra = jnp.exp(jnp.where(causal, g_row - mm_col[u], -jnp.inf))
        s[u] = lax.dot_general(q[u], k[u], nt, preferred_element_type=F32) * w_intra
    intra = {u: jnp.dot(s[u].astype(BF16), v_ext[u][:, 0:MLSTM_DV], preferred_element_type=F32)
             for u in units}
    upd = {u: lax.dot_general((k[u].astype(F32) * w_state[u]).astype(BF16), v_ext[u], tn,
                              preferred_element_type=F32) for u in units}

    c_in = {}
    for h in range(nh):
        state = c_ref[h]
        for c in range(n_chunks):
            c_in[c, h] = state.astype(BF16)
            state = decay[c, h] * state + upd[c, h]
        c_ref[h] = state

    for u in units:
        c, h = u
        inter = jnp.dot(q[u], c_in[u], preferred_element_type=F32)
        num = intra[u] + w_inter[u] * inter[:, 0:MLSTM_DV]
        nq = jnp.sum(s[u], axis=-1, keepdims=True) + w_inter[u] * inter[:, MLSTM_DV:MLSTM_DV + 1]
        b_col = col[c][:, COL_B + h:COL_B + h + 1]
        hh = num / jnp.maximum(jnp.abs(nq), jnp.exp(-(b_col + mm_col[u])))
        hn = hh * lax.rsqrt(jnp.mean(hh * hh, axis=-1, keepdims=True) + NORM_EPS) * hg_ref[:, dvs[h]]
        y_ref[rows[c], dvs[h]] = (o_ref[rows[c], dvs[h]] * hn).astype(BF16)


def _mlstm(l, qk, v, o_sig, cols, head_g):
    b, s, _ = qk.shape
    t = _pick_tile(s, 256)
    row = lambda i, j: (i, j, 0)
    return pl.pallas_call(
        _mlstm_kernel,
        grid=(b, s // t),
        in_specs=[
            pl.BlockSpec((None, t, QK_W), row),
            pl.BlockSpec((None, t, V_W), row),
            pl.BlockSpec((None, t, V_W), row),
            pl.BlockSpec((None, t, LANES), row),
            pl.BlockSpec((None, 1, V_W), lambda i, j: (l, 0, 0)),
        ],
        out_specs=pl.BlockSpec((None, t, V_W), row),
        out_shape=jax.ShapeDtypeStruct((b, s, V_W), BF16),
        scratch_shapes=[pltpu.VMEM((MLSTM_HEADS, MLSTM_DQK, MLSTM_DV + LANES), F32),
                        pltpu.VMEM((SUBLANES, LANES), F32)],
        compiler_params=_params("arbitrary", "arbitrary"),
        name="mlstm",
    )(qk, v, o_sig, cols, head_g)


def _block_diag(band, kv_head):
    lane = lax.broadcasted_iota(jnp.int32, band.shape, 1)
    low = lane < ATTN_HEAD_DIM
    rolled = pltpu.roll(band, ATTN_HEAD_DIM, 1)
    if kv_head == 0:
        top = jnp.where(low, band, 0.0)
        bot = jnp.where(low, 0.0, rolled)
    else:
        top = jnp.where(low, rolled, 0.0)
        bot = jnp.where(low, 0.0, band)
    return jnp.concatenate([top, bot], axis=0).astype(BF16)


def _attn_kernel(q_ref, kc_ref, kp_ref, vc_ref, vp_ref, sink_ref, y_ref):
    W = WINDOW
    n_blocks = q_ref.shape[0] // W
    prev_penalty = jnp.where(pl.program_id(1) == 0, -jnp.inf, 0.0)

    key = lax.broadcasted_iota(jnp.int32, (W, PAIRS * W), 0)
    qry = lax.broadcasted_iota(jnp.int32, (W, PAIRS * W), 1) % W
    use_prev = key > qry

    def softmax_t(prev, cur, sink):
        comb = jnp.where(use_prev, prev, cur)
        m = jnp.maximum(jnp.max(comb, axis=0, keepdims=True), sink)
        p = jnp.exp(comb - m)
        den = jnp.sum(p, axis=0, keepdims=True) + jnp.exp(sink - m)
        p = p * (1.0 / den)
        return jnp.where(use_prev, p, 0.0), jnp.where(use_prev, 0.0, p)

    for blk in range(n_blocks):
        rows = slice(blk * W, (blk + 1) * W)
        if blk == 0:
            k_prev, v_prev = kp_ref[...], vp_ref[...]
        else:
            prev = slice((blk - 1) * W, blk * W)
            k_prev, v_prev = kc_ref[prev, :], vc_ref[prev, :]
        k_band = jnp.concatenate([k_prev, kc_ref[rows, :]], axis=0)
        v_band = jnp.concatenate([v_prev, vc_ref[rows, :]], axis=0)
        for kvh in range(ATTN_KV_HEADS):
            k_bd = _block_diag(k_band, kvh)
            v_bd = _block_diag(v_band, kvh)
            lanes = [slice((kvh * PAIRS + p) * LANES, (kvh * PAIRS + p + 1) * LANES)
                     for p in range(PAIRS)]
            q_all = jnp.concatenate([q_ref[rows, ln] for ln in lanes], axis=0)
            st = lax.dot_general(k_bd, q_all, (((1,), (1,)), ((), ())),
                                 preferred_element_type=F32)
            prev_e, cur_e, prev_o, cur_o = (st[i * W:(i + 1) * W, :] for i in range(4))
            if blk == 0:
                prev_e = prev_e + prev_penalty
                prev_o = prev_o + prev_penalty
            pe_prev, pe_cur = softmax_t(prev_e, cur_e, sink_ref[2 * kvh:2 * kvh + 1, :])
            po_prev, po_cur = softmax_t(prev_o, cur_o, sink_ref[2 * kvh + 1:2 * kvh + 2, :])
            pt = jnp.concatenate([pe_prev, pe_cur, po_prev, po_cur], axis=0).astype(BF16)
            out_t = lax.dot_general(v_bd, pt, (((0,), (0,)), ((), ())),
                                    preferred_element_type=F32)
            for p in range(PAIRS):
                y_ref[rows, lanes[p]] = out_t[:, p * W:(p + 1) * W].T.astype(BF16)


def _attention(l, qa, ka, va, sink_rows):
    b, s, _ = qa.shape
    tq = _pick_tile(s, 512)
    per = tq // WINDOW
    cur = lambda i, j: (i, j, 0)
    prev = lambda i, j: (i, jnp.maximum(j * per - 1, 0), 0)
    return pl.pallas_call(
        _attn_kernel,
        grid=(b, s // tq),
        in_specs=[
            pl.BlockSpec((None, tq, QA_W), cur),
            pl.BlockSpec((None, tq, KV_W), cur),
            pl.BlockSpec((None, WINDOW, KV_W), prev),
            pl.BlockSpec((None, tq, KV_W), cur),
            pl.BlockSpec((None, WINDOW, KV_W), prev),
            pl.BlockSpec((None, 2 * ATTN_KV_HEADS, PAIRS * WINDOW), lambda i, j: (l, 0, 0)),
        ],
        out_specs=pl.BlockSpec((None, tq, QA_W), cur),
        out_shape=jax.ShapeDtypeStruct((b, s, QA_W), BF16),
        compiler_params=_params("arbitrary", "arbitrary"),
        name="swa_attention",
    )(qa, ka, ka, va, va, sink_rows)


def _outproj_kernel(ym_ref, ya_ref, x_ref, gate_ref, w_ref, o_ref):
    km = ym_ref.shape[1]
    y = jnp.dot(ym_ref[...], w_ref[0:km, :], preferred_element_type=F32)
    y = y + jnp.dot(ya_ref[...], w_ref[km:, :], preferred_element_type=F32)
    o_ref[...] = x_ref[...] + gate_ref[...] * y


def _outproj(l, ym, ya, x, mod, w_out):
    b, s, d = x.shape
    tm = _pick_tile(s, 512)
    row = lambda i, j: (i, j, 0)
    return pl.pallas_call(
        _outproj_kernel,
        grid=(b, s // tm),
        in_specs=[
            pl.BlockSpec((None, tm, ym.shape[2]), row),
            pl.BlockSpec((None, tm, ya.shape[2]), row),
            pl.BlockSpec((None, tm, d), row),
            pl.BlockSpec((None, None, None, 1, d), lambda i, j: (l, i, 2, 0, 0)),
            pl.BlockSpec((None,) + w_out.shape[1:], lambda i, j: (l, 0, 0)),
        ],
        out_specs=pl.BlockSpec((None, tm, d), row),
        out_shape=jax.ShapeDtypeStruct((b, s, d), F32),
        compiler_params=_params("arbitrary", "arbitrary"),
        name="out_proj",
    )(ym, ya, x, mod, w_out)


def _ffn_kernel(x_ref, shift_ref, scale_ref, gate_ref, g_ref, fg_ref, wg_ref, wu_ref, cw_ref, wd_ref,
                o_ref, hb_ref, gs0_ref, gs1_ref, us0_ref, us1_ref, act_ref, tail_ref, *, nf,
                final_norm):
    gs_ref = (gs0_ref, gs1_ref)
    us_ref = (us0_ref, us1_ref)
    tm = x_ref.shape[0]
    tf = wg_ref.shape[1]
    halo = SUBLANES
    i = pl.program_id(1)
    j = pl.program_id(2)

    def up_g(slot):
        gs_ref[slot][halo:halo + tm, :] = jnp.dot(hb_ref[...], wg_ref[...],
                                                  preferred_element_type=F32)

    def up_u(slot):
        us_ref[slot][...] = jnp.dot(hb_ref[...], wu_ref[...], preferred_element_type=F32)

    def fill_halo(slot):
        @pl.when(i == 0)
        def _():
            gs_ref[slot][0:halo, :] = jnp.zeros((halo, tf), F32)

        @pl.when(i > 0)
        def _():
            gs_ref[slot][0:halo, :] = tail_ref[j - 1]

    def gated(slot):
        cw = cw_ref[...]
        gsr = gs_ref[slot]
        conv = gsr[halo - 2:halo - 2 + tm, :] * cw[0:1, :]
        conv = conv + gsr[halo - 1:halo - 1 + tm, :] * cw[1:2, :]
        conv = conv + gsr[halo:halo + tm, :] * cw[2:3, :]
        tail_ref[j - 1] = gsr[tm:tm + halo, :]
        act_ref[...] = (jax.nn.silu(conv) * us_ref[slot][...]).astype(BF16)

    def down():
        o_ref[...] += jnp.dot(act_ref[...], wd_ref[...], preferred_element_type=F32)

    @pl.when(j == 0)
    def _():
        hb_ref[...] = _adaln(x_ref[...], g_ref[...], shift_ref[...], scale_ref[...]).astype(BF16)
        o_ref[...] = jnp.zeros(o_ref.shape, F32)
        up_g(0)
        up_u(0)

    for par in range(2):
        @pl.when((j > 0) & (j < nf) & (lax.rem(j, 2) == par))
        def _():
            fill_halo(1 - par)
            up_g(par)
            gated(1 - par)
            down()
            up_u(par)

    @pl.when(j == nf)
    def _():
        slot = (nf - 1) % 2
        fill_halo(slot)
        gated(slot)
        down()
        res = x_ref[...] + gate_ref[...] * o_ref[...]
        if final_norm:
            ms = jnp.mean(res * res, axis=-1, keepdims=True)
            res = res * lax.rsqrt(ms + NORM_EPS) * fg_ref[...]
        o_ref[...] = res


def _ffn(l, x, mod, g, final_g, w_up, conv_ffn, w_down, final_norm):
    b, s, d = x.shape
    ff = w_down.shape[1]
    tm = _pick_tile(s, 512)
    tf = _pick_tile(ff, 512)
    nf = ff // tf
    row = lambda i, t, j: (i, t, 0)
    mod_spec = lambda k: pl.BlockSpec((None, None, None, 1, d), lambda i, t, j: (l, i, k, 0, 0))
    up_tile = lambda j: jnp.minimum(j, nf - 1)
    down_tile = lambda j: jnp.maximum(j - 1, 0)
    return pl.pallas_call(
        functools.partial(_ffn_kernel, nf=nf, final_norm=final_norm),
        grid=(b, s // tm, nf + 1),
        in_specs=[
            pl.BlockSpec((None, tm, d), row),
            mod_spec(3), mod_spec(4), mod_spec(5),
            pl.BlockSpec((None, 1, d), lambda i, t, j: (l, 0, 0)),
            pl.BlockSpec((1, d), lambda i, t, j: (0, 0)),
            pl.BlockSpec((None, d, tf), lambda i, t, j: (l, 0, up_tile(j))),
            pl.BlockSpec((None, d, tf), lambda i, t, j: (l, 0, nf + up_tile(j))),
            pl.BlockSpec((None, FFN_CONV, tf), lambda i, t, j: (l, 0, down_tile(j))),
            pl.BlockSpec((None, tf, d), lambda i, t, j: (l, down_tile(j), 0)),
        ],
        out_specs=pl.BlockSpec((None, tm, d), row),
        out_shape=jax.ShapeDtypeStruct((b, s, d), F32),
        scratch_shapes=[pltpu.VMEM((tm, d), BF16),
                        pltpu.VMEM((tm + SUBLANES, tf), F32),
                        pltpu.VMEM((tm + SUBLANES, tf), F32),
                        pltpu.VMEM((tm, tf), F32),
                        pltpu.VMEM((tm, tf), F32),
                        pltpu.VMEM((tm, tf), BF16),
                        pltpu.VMEM((nf, SUBLANES, tf), F32)],
        compiler_params=_params("arbitrary", "arbitrary", "arbitrary"),
        name="convglu_ffn",
    )(x, mod, mod, mod, g, final_g, w_up, w_up, conv_ffn, w_down)


def _pack_w_in(w_in):
    nh = MLSTM_HEADS
    o_gates = QK_W + 2 * V_W
    o_attn = o_gates + 2 * nh
    gates = w_in[:, :, o_gates:o_attn]
    gates = jnp.pad(gates, ((0, 0), (0, 0), (0, LANES - 2 * nh)))
    packed = jnp.concatenate([w_in[:, :, :o_gates], w_in[:, :, o_attn:], gates], axis=-1)
    return packed.astype(BF16)


def _sink_rows(sinks):
    depth = sinks.shape[0]
    t = sinks.reshape(depth, ATTN_KV_HEADS, PAIRS, 2).transpose(0, 1, 3, 2)
    t = jnp.repeat(t[..., None], WINDOW, axis=-1)
    return t.reshape(depth, 2 * ATTN_KV_HEADS, PAIRS * WINDOW)


def kernel(x, c, positions, ada_w, ada_b, norm_mix_g, w_in, b_gates, conv_qk, mlstm_head_g, sinks,
           w_out, norm_ffn_g, w_up, conv_ffn, w_down, final_g):
    b, s, d = x.shape
    depth = ada_w.shape[0]
    nh = MLSTM_HEADS

    c_pad = jnp.pad(c, ((0, 2 * SUBLANES - b), (0, 0)))
    mod = _modulation(c_pad, ada_w, ada_b)[:, :b].reshape(depth, b, 6, 1, d)
    cos, sina, sinb = _rope_tables(positions)

    w_in_p = _pack_w_in(w_in)
    w_out_b = w_out.astype(BF16)
    w_up_b = w_up.astype(BF16)
    w_down_b = w_down.astype(BF16)
    bg = jnp.pad(b_gates, ((0, 0), (0, LANES - 2 * nh))).reshape(depth, 1, LANES)
    norm_mix = norm_mix_g.reshape(depth, 1, d)
    norm_ffn = norm_ffn_g.reshape(depth, 1, d)
    head_g = mlstm_head_g.reshape(depth, 1, V_W)
    sink_rows = _sink_rows(sinks)
    final_g2 = final_g.reshape(1, d)

    for l in range(depth):
        qk, v, o_sig, cols, qa, ka, va = _inproj(
            l, x, mod, norm_mix, w_in_p, conv_qk, bg, cos, sina, sinb)
        ym = _mlstm(l, qk, v, o_sig, cols, head_g)
        ya = _attention(l, qa, ka, va, sink_rows)
        x = _outproj(l, ym, ya, x, mod, w_out_b)
        x = _ffn(l, x, mod, norm_ffn, final_g2, w_up_b, conv_ffn, w_down_b,
                 final_norm=(l == depth - 1))
    return x
```

```python
import functools

import jax
import jax.numpy as jnp
from jax import lax
from jax.experimental import pallas as pl
from jax.experimental.pallas import tpu as pltpu

F32 = jnp.float32
BF16 = jnp.bfloat16

MLSTM_HEADS = 4
MLSTM_DQK = 128
MLSTM_DV = 256
MLSTM_CONV = 4
MLSTM_CHUNK = 256
GATE_SOFTCAP = 15.0
ATTN_HEAD_DIM = 64
ATTN_HEADS = 16
ATTN_KV_HEADS = 2
WINDOW = 128
ROPE_DIM = 16
ROPE_THETA = 500000.0
FFN_CONV = 3
NORM_EPS = 1e-6

LANES = 128
SUBLANES = 8
VMEM_LIMIT = 52 * 1024 * 1024

QK_W = 2 * MLSTM_HEADS * MLSTM_DQK
V_W = MLSTM_HEADS * MLSTM_DV
QA_W = ATTN_HEADS * ATTN_HEAD_DIM
KV_W = ATTN_KV_HEADS * ATTN_HEAD_DIM
OFF_QK = 0
OFF_V = OFF_QK + QK_W
OFF_O = OFF_V + V_W
OFF_QA = OFF_O + V_W
OFF_KA = OFF_QA + QA_W
OFF_VA = OFF_KA + KV_W
OFF_G = OFF_VA + KV_W
IN_PACKED = OFF_G + LANES

COL_G = 0
COL_B = MLSTM_HEADS
COL_CG = 2 * MLSTM_HEADS
PAIRS = ATTN_HEADS // ATTN_KV_HEADS // 2


def _params(*sem):
    return pltpu.CompilerParams(dimension_semantics=sem, vmem_limit_bytes=VMEM_LIMIT)


def _pick_tile(n, pref):
    t = min(n, pref)
    while n % t:
        t //= 2
    return t


def _ffn_tiles(s, ff):
    return _pick_tile(s, 1024), _pick_tile(ff, 256)


def _mod_kernel(c_ref, w_ref, b_ref, o_ref):
    ca = jax.nn.silu(c_ref[...]).astype(BF16)
    o_ref[...] = jnp.dot(ca, w_ref[...].astype(BF16), preferred_element_type=F32) + b_ref[...]


def _modulation(c_pad, ada_w, ada_b):
    depth, d, n = ada_w.shape
    rows = c_pad.shape[0]
    tn = _pick_tile(n, 1024)
    return pl.pallas_call(
        _mod_kernel,
        grid=(depth, n // tn),
        in_specs=[
            pl.BlockSpec((rows, d), lambda l, j: (0, 0)),
            pl.BlockSpec((None, d, tn), lambda l, j: (l, 0, j)),
            pl.BlockSpec((None, 1, tn), lambda l, j: (l, 0, j)),
        ],
        out_specs=pl.BlockSpec((None, rows, tn), lambda l, j: (l, 0, j)),
        out_shape=jax.ShapeDtypeStruct((depth, rows, n), F32),
        compiler_params=_params("arbitrary", "arbitrary"),
        name="adaln_mod",
    )(c_pad, ada_w, ada_b.reshape(depth, 1, n))


def _rope_kernel(pos_ref, freq_ref, cos_ref, sina_ref, sinb_ref):
    ang = pos_ref[...].astype(F32) * freq_ref[...]
    d = lax.broadcasted_iota(jnp.int32, ang.shape, 1) % ATTN_HEAD_DIM
    c = jnp.cos(ang)
    s = jnp.sin(ang)
    half = ROPE_DIM // 2
    cos_ref[...] = jnp.where(d < ROPE_DIM, c, 1.0)
    sina_ref[...] = jnp.where(d < half, -s, 0.0)
    sinb_ref[...] = jnp.where((d >= half) & (d < ROPE_DIM), s, 0.0)


def _rope_tables(positions):
    b, s = positions.shape
    ts = _pick_tile(s, 1024)
    inv_freq = ROPE_THETA ** (-jnp.arange(0, ROPE_DIM, 2, dtype=F32) / ROPE_DIM)
    per_head = jnp.concatenate(
        [inv_freq, inv_freq, jnp.zeros((ATTN_HEAD_DIM - ROPE_DIM,), F32)])
    freq = jnp.tile(per_head, LANES // ATTN_HEAD_DIM).reshape(1, LANES)
    tab = jax.ShapeDtypeStruct((b, s, LANES), F32)
    spec = pl.BlockSpec((None, ts, LANES), lambda i, j: (i, j, 0))
    return pl.pallas_call(
        _rope_kernel,
        grid=(b, s // ts),
        in_specs=[pl.BlockSpec((None, ts, 1), lambda i, j: (i, j, 0)),
                  pl.BlockSpec((1, LANES), lambda i, j: (0, 0))],
        out_specs=[spec, spec, spec],
        out_shape=[tab, tab, tab],
        compiler_params=_params("arbitrary", "arbitrary"),
        name="rope_tables",
    )(positions.reshape(b, s, 1), freq)


def _adaln(x, g, shift, scale):
    ms = jnp.mean(x * x, axis=-1, keepdims=True)
    h = x * lax.rsqrt(ms + NORM_EPS) * g
    return h * (1.0 + scale) + shift


def _rope(t, cos, sina, sinb):
    n = t.shape[-1]
    half = ROPE_DIM // 2
    return t * cos + pltpu.roll(t, n - half, 1) * sina + pltpu.roll(t, half, 1) * sinb


def _chunk_scan_rows(x, op, fill):
    pos = lax.broadcasted_iota(jnp.int32, x.shape, 0) % MLSTM_CHUNK
    step = 1
    while step < MLSTM_CHUNK:
        x = op(x, jnp.where(pos >= step, pltpu.roll(x, step, 0), fill))
        step *= 2
    return x


def _gate_columns(gates):
    nh = MLSTM_HEADS
    lane = lax.broadcasted_iota(jnp.int32, gates.shape, 1)
    b = _chunk_scan_rows(jax.nn.log_sigmoid(gates), jnp.add, 0.0)
    g = gates - pltpu.roll(b, LANES - nh, 1)
    cg = _chunk_scan_rows(g, jnp.maximum, -jnp.inf)
    return jnp.where(lane < COL_B, g, jnp.where(lane < COL_CG, b, pltpu.roll(cg, COL_CG, 1)))


def _inproj_kernel(x_ref, shift_ref, scale_ref, g_ref, w_ref, cw_ref, bg_ref,
                   cos_ref, sina_ref, sinb_ref,
                   qk_ref, v_ref, o_ref, col_ref, qa_ref, ka_ref, va_ref,
                   hb_ref, cbuf_ref):
    tm = x_ref.shape[0]
    halo = SUBLANES

    @pl.when(pl.program_id(1) == 0)
    def _():
        cbuf_ref[0:halo, :] = jnp.zeros((halo, QK_W), F32)

    hb_ref[...] = _adaln(x_ref[...], g_ref[...], shift_ref[...], scale_ref[...]).astype(BF16)

    def proj(off, width):
        return jnp.dot(hb_ref[...], w_ref[:, off:off + width], preferred_element_type=F32)

    pg = proj(OFF_G, LANES) + bg_ref[...]
    col_ref[...] = _gate_columns(GATE_SOFTCAP * jnp.tanh(pg / GATE_SOFTCAP))

    cw = cw_ref[...]
    half = QK_W // 2
    for c in range(2):
        lo = c * half
        pq = proj(OFF_QK + lo, half)
        cbuf_ref[halo:halo + tm, lo:lo + half] = pq
        acc = cbuf_ref[halo - 3:halo - 3 + tm, lo:lo + half] * cw[0:1, lo:lo + half]
        acc = acc + cbuf_ref[halo - 2:halo - 2 + tm, lo:lo + half] * cw[1:2, lo:lo + half]
        acc = acc + cbuf_ref[halo - 1:halo - 1 + tm, lo:lo + half] * cw[2:3, lo:lo + half]
        acc = acc + pq * cw[3:4, lo:lo + half]
        act = jax.nn.silu(acc)
        if c == 0:
            act = act * (MLSTM_DQK ** -0.5)
        qk_ref[:, lo:lo + half] = act.astype(BF16)
        vlo = c * (V_W // 2)
        v_ref[:, vlo:vlo + V_W // 2] = proj(OFF_V + vlo, V_W // 2).astype(BF16)
    cbuf_ref[0:halo, :] = cbuf_ref[tm:tm + halo, :]

    for c in range(2):
        lo = c * (V_W // 2)
        o_ref[:, lo:lo + V_W // 2] = jax.nn.sigmoid(proj(OFF_O + lo, V_W // 2))

    cos = cos_ref[...]
    sina = sina_ref[...]
    sinb = sinb_ref[...]
    for c in range(QA_W // 256):
        pa = proj(OFF_QA + c * 256, 256)
        for k in range(2):
            lo = c * 256 + k * LANES
            r = _rope(pa[:, k * LANES:(k + 1) * LANES], cos, sina, sinb)
            qa_ref[:, lo:lo + LANES] = (r * (ATTN_HEAD_DIM ** -0.5)).astype(BF16)

    pkv = proj(OFF_KA, 2 * KV_W)
    ka_ref[...] = _rope(pkv[:, 0:KV_W], cos, sina, sinb)
    va_ref[...] = pkv[:, KV_W:2 * KV_W]


def _inproj(l, x, mod, g, w_packed, conv_qk, bg, cos, sina, sinb):
    b, s, d = x.shape
    tm = _pick_tile(s, 512)
    row = lambda i, j: (i, j, 0)
    lay = lambda i, j: (l, 0, 0)
    outs = [
        (QK_W, BF16), (V_W, BF16), (V_W, F32), (LANES, F32), (QA_W, BF16), (KV_W, F32), (KV_W, F32),
    ]
    return pl.pallas_call(
        _inproj_kernel,
        grid=(b, s // tm),
        in_specs=[
            pl.BlockSpec((None, tm, d), row),
            pl.BlockSpec((None, None, None, 1, d), lambda i, j: (l, i, 0, 0, 0)),
            pl.BlockSpec((None, None, None, 1, d), lambda i, j: (l, i, 1, 0, 0)),
            pl.BlockSpec((None, 1, d), lay),
            pl.BlockSpec((None, d, IN_PACKED), lay),
            pl.BlockSpec((None, MLSTM_CONV, QK_W), lay),
            pl.BlockSpec((None, 1, LANES), lay),
            pl.BlockSpec((None, tm, LANES), row),
            pl.BlockSpec((None, tm, LANES), row),
            pl.BlockSpec((None, tm, LANES), row),
        ],
        out_specs=[pl.BlockSpec((None, tm, w), row) for w, _ in outs],
        out_shape=[jax.ShapeDtypeStruct((b, s, w), dt) for w, dt in outs],
        scratch_shapes=[pltpu.VMEM((tm, d), BF16), pltpu.VMEM((tm + SUBLANES, QK_W), F32)],
        compiler_params=_params("arbitrary", "arbitrary"),
        name="in_proj",
    )(x, mod, mod, g, w_packed, conv_qk, bg, cos, sina, sinb)


def _mlstm_kernel(qk_ref, v_ref, o_ref, col_ref, hg_ref, y_ref, c_ref, m_ref):
    L = MLSTM_CHUNK
    nh = MLSTM_HEADS
    n_chunks = qk_ref.shape[0] // L

    @pl.when(pl.program_id(1) == 0)
    def _():
        c_ref[...] = jnp.zeros(c_ref.shape, F32)
        m_ref[...] = jnp.zeros(m_ref.shape, F32)

    ri = lax.broadcasted_iota(jnp.int32, (L, L), 0)
    ci = lax.broadcasted_iota(jnp.int32, (L, L), 1)
    causal = ci <= ri
    ones_col = (lax.broadcasted_iota(jnp.int32, (L, LANES), 1) == 0).astype(BF16)
    nt = (((1,), (1,)), ((), ()))
    tn = (((0,), (0,)), ((), ()))
    units = [(c, h) for c in range(n_chunks) for h in range(nh)]
    rows = [slice(c * L, (c + 1) * L) for c in range(n_chunks)]
    dvs = [slice(h * MLSTM_DV, (h + 1) * MLSTM_DV) for h in range(nh)]

    col = [col_ref[r, :] for r in rows]
    col_t = [cl.T for cl in col]
    q = {(c, h): qk_ref[rows[c], h * MLSTM_DQK:(h + 1) * MLSTM_DQK] for c, h in units}
    k = {(c, h): qk_ref[rows[c], (nh + h) * MLSTM_DQK:(nh + h + 1) * MLSTM_DQK] for c, h in units}
    v_ext = {(c, h): jnp.concatenate([v_ref[rows[c], dvs[h]], ones_col], axis=1) for c, h in units}

    mm_col, w_inter, decay, w_state = {}, {}, {}, {}
    for h in range(nh):
        m = m_ref[h:h + 1, 0:1]
        for c in range(n_chunks):
            mm = jnp.maximum(m, col[c][:, COL_CG + h:COL_CG + h + 1])
            mm_last = mm[L - 1:L, :]
            mm_col[c, h] = mm
            w_inter[c, h] = jnp.exp(m - mm)
            decay[c, h] = jnp.exp(m - mm_last)
            w_state[c, h] = jnp.exp(col[c][:, COL_G + h:COL_G + h + 1] - mm_last)
            m = col[c][L - 1:L, COL_B + h:COL_B + h + 1] + mm_last
        m_ref[h:h + 1, :] = jnp.broadcast_to(m, (1, LANES))

    s = {}
    for u in units:
        c, h = u
        g_row = col_t[c][COL_G + h:COL_G + h + 1, :]
        w_intra = jnp.exp(jnp.where(causal, g_row - mm_col[u], -jnp.inf))
        s[u] = lax.dot_general(q[u], k[u], nt, preferred_element_type=F32) * w_intra
    intra = {u: jnp.dot(s[u].astype(BF16), v_ext[u][:, 0:MLSTM_DV], preferred_element_type=F32)
             for u in units}
    upd = {u: lax.dot_general((k[u].astype(F32) * w_state[u]).astype(BF16), v_ext[u], tn,
                              preferred_element_type=F32) for u in units}

    c_in = {}
    for h in range(nh):
        state = c_ref[h]
        for c in range(n_chunks):
            c_in[c, h] = state.astype(BF16)
            state = decay[c, h] * state + upd[c, h]
        c_ref[h] = state

    for u in units:
        c, h = u
        inter = jnp.dot(q[u], c_in[u], preferred_element_type=F32)
        num = intra[u] + w_inter[u] * inter[:, 0:MLSTM_DV]
        nq = jnp.sum(s[u], axis=-1, keepdims=True) + w_inter[u] * inter[:, MLSTM_DV:MLSTM_DV + 1]
        b_col = col[c][:, COL_B + h:COL_B + h + 1]
        hh = num / jnp.maximum(jnp.abs(nq), jnp.exp(-(b_col + mm_col[u])))
        hn = hh * lax.rsqrt(jnp.mean(hh * hh, axis=-1, keepdims=True) + NORM_EPS) * hg_ref[:, dvs[h]]
        y_ref[rows[c], dvs[h]] = (o_ref[rows[c], dvs[h]] * hn).astype(BF16)


def _mlstm(l, qk, v, o_sig, cols, head_g):
    b, s, _ = qk.shape
    t = MLSTM_CHUNK
    assert s % t == 0, (s, t)
    row = lambda i, j: (i, j, 0)
    return pl.pallas_call(
        _mlstm_kernel,
        grid=(b, s // t),
        in_specs=[
            pl.BlockSpec((None, t, QK_W), row),
            pl.BlockSpec((None, t, V_W), row),
            pl.BlockSpec((None, t, V_W), row),
            pl.BlockSpec((None, t, LANES), row),
            pl.BlockSpec((None, 1, V_W), lambda i, j: (l, 0, 0)),
        ],
        out_specs=pl.BlockSpec((None, t, V_W), row),
        out_shape=jax.ShapeDtypeStruct((b, s, V_W), BF16),
        scratch_shapes=[pltpu.VMEM((MLSTM_HEADS, MLSTM_DQK, MLSTM_DV + LANES), F32),
                        pltpu.VMEM((SUBLANES, LANES), F32)],
        compiler_params=_params("arbitrary", "arbitrary"),
        name="mlstm",
    )(qk, v, o_sig, cols, head_g)


def _block_diag(band, kv_head):
    lane = lax.broadcasted_iota(jnp.int32, band.shape, 1)
    low = lane < ATTN_HEAD_DIM
    rolled = pltpu.roll(band, ATTN_HEAD_DIM, 1)
    if kv_head == 0:
        top = jnp.where(low, band, 0.0)
        bot = jnp.where(low, 0.0, rolled)
    else:
        top = jnp.where(low, rolled, 0.0)
        bot = jnp.where(low, 0.0, band)
    return jnp.concatenate([top, bot], axis=0).astype(BF16)


def _attn_kernel(q_ref, kc_ref, kp_ref, vc_ref, vp_ref, sink_ref, y_ref):
    W = WINDOW
    n_blocks = q_ref.shape[0] // W
    prev_penalty = jnp.where(pl.program_id(1) == 0, -jnp.inf, 0.0)

    key = lax.broadcasted_iota(jnp.int32, (W, PAIRS * W), 0)
    qry = lax.broadcasted_iota(jnp.int32, (W, PAIRS * W), 1) % W
    use_prev = key > qry

    def softmax_t(prev, cur, sink):
        comb = jnp.where(use_prev, prev, cur)
        m = jnp.maximum(jnp.max(comb, axis=0, keepdims=True), sink)
        p = jnp.exp(comb - m)
        den = jnp.sum(p, axis=0, keepdims=True) + jnp.exp(sink - m)
        p = p * (1.0 / den)
        return jnp.where(use_prev, p, 0.0), jnp.where(use_prev, 0.0, p)

    for blk in range(n_blocks):
        rows = slice(blk * W, (blk + 1) * W)
        if blk == 0:
            k_prev, v_prev = kp_ref[...], vp_ref[...]
        else:
            prev = slice((blk - 1) * W, blk * W)
            k_prev, v_prev = kc_ref[prev, :], vc_ref[prev, :]
        k_band = jnp.concatenate([k_prev, kc_ref[rows, :]], axis=0)
        v_band = jnp.concatenate([v_prev, vc_ref[rows, :]], axis=0)
        for kvh in range(ATTN_KV_HEADS):
            k_bd = _block_diag(k_band, kvh)
            v_bd = _block_diag(v_band, kvh)
            lanes = [slice((kvh * PAIRS + p) * LANES, (kvh * PAIRS + p + 1) * LANES)
                     for p in range(PAIRS)]
            q_all = jnp.concatenate([q_ref[rows, ln] for ln in lanes], axis=0)
            st = lax.dot_general(k_bd, q_all, (((1,), (1,)), ((), ())),
                                 preferred_element_type=F32)
            prev_e, cur_e, prev_o, cur_o = (st[i * W:(i + 1) * W, :] for i in range(4))
            if blk == 0:
                prev_e = prev_e + prev_penalty
                prev_o = prev_o + prev_penalty
            pe_prev, pe_cur = softmax_t(prev_e, cur_e, sink_ref[2 * kvh:2 * kvh + 1, :])
            po_prev, po_cur = softmax_t(prev_o, cur_o, sink_ref[2 * kvh + 1:2 * kvh + 2, :])
            pt = jnp.concatenate([pe_prev, pe_cur, po_prev, po_cur], axis=0).astype(BF16)
            out_t = lax.dot_general(v_bd, pt, (((0,), (0,)), ((), ())),
                                    preferred_element_type=F32)
            for p in range(PAIRS):
                y_ref[rows, lanes[p]] = out_t[:, p * W:(p + 1) * W].T.astype(BF16)


def _attention(l, qa, ka, va, sink_rows):
    b, s, _ = qa.shape
    tq = _pick_tile(s, 512)
    per = tq // WINDOW
    cur = lambda i, j: (i, j, 0)
    prev = lambda i, j: (i, jnp.maximum(j * per - 1, 0), 0)
    return pl.pallas_call(
        _attn_kernel,
        grid=(b, s // tq),
        in_specs=[
            pl.BlockSpec((None, tq, QA_W), cur),
            pl.BlockSpec((None, tq, KV_W), cur),
            pl.BlockSpec((None, WINDOW, KV_W), prev),
            pl.BlockSpec((None, tq, KV_W), cur),
            pl.BlockSpec((None, WINDOW, KV_W), prev),
            pl.BlockSpec((None, 2 * ATTN_KV_HEADS, PAIRS * WINDOW), lambda i, j: (l, 0, 0)),
        ],
        out_specs=pl.BlockSpec((None, tq, QA_W), cur),
        out_shape=jax.ShapeDtypeStruct((b, s, QA_W), BF16),
        compiler_params=_params("arbitrary", "arbitrary"),
        name="swa_attention",
    )(qa, ka, ka, va, va, sink_rows)


def _outproj_kernel(ym_ref, ya_ref, x_ref, gate_ref, w_ref, o_ref):
    km = ym_ref.shape[1]
    y = jnp.dot(ym_ref[...], w_ref[0:km, :], preferred_element_type=F32)
    y = y + jnp.dot(ya_ref[...], w_ref[km:, :], preferred_element_type=F32)
    o_ref[...] = x_ref[...] + gate_ref[...] * y


def _outproj(l, ym, ya, x, mod, w_out):
    b, s, d = x.shape
    tm = _pick_tile(s, 512)
    row = lambda i, j: (i, j, 0)
    return pl.pallas_call(
        _outproj_kernel,
        grid=(b, s // tm),
        in_specs=[
            pl.BlockSpec((None, tm, ym.shape[2]), row),
            pl.BlockSpec((None, tm, ya.shape[2]), row),
            pl.BlockSpec((None, tm, d), row),
            pl.BlockSpec((None, None, None, 1, d), lambda i, j: (l, i, 2, 0, 0)),
            pl.BlockSpec((None,) + w_out.shape[1:], lambda i, j: (l, 0, 0)),
        ],
        out_specs=pl.BlockSpec((None, tm, d), row),
        out_shape=jax.ShapeDtypeStruct((b, s, d), F32),
        compiler_params=_params("arbitrary", "arbitrary"),
        name="out_proj",
    )(ym, ya, x, mod, w_out)


def _ffn_kernel(x_ref, shift_ref, scale_ref, gate_ref, g_ref, fg_ref, wg_ref, wu_ref, cw_ref, wd_ref,
                o_ref, hb_ref, gs0_ref, gs1_ref, us0_ref, us1_ref, act_ref, tail_ref, *, nf,
                final_norm):
    gs_ref = (gs0_ref, gs1_ref)
    us_ref = (us0_ref, us1_ref)
    tm = x_ref.shape[0]
    tf = wg_ref.shape[1]
    halo = SUBLANES
    i = pl.program_id(1)
    j = pl.program_id(2)

    def up_g(slot):
        gs_ref[slot][halo:halo + tm, :] = jnp.dot(hb_ref[...], wg_ref[...],
                                                  preferred_element_type=F32)

    def up_u(slot):
        us_ref[slot][...] = jnp.dot(hb_ref[...], wu_ref[...], preferred_element_type=F32)

    def fill_halo(slot):
        @pl.when(i == 0)
        def _():
            gs_ref[slot][0:halo, :] = jnp.zeros((halo, tf), F32)

        @pl.when(i > 0)
        def _():
            gs_ref[slot][0:halo, :] = tail_ref[j - 1]

    def gated(slot):
        cw = cw_ref[...]
        gsr = gs_ref[slot]
        conv = gsr[halo - 2:halo - 2 + tm, :] * cw[0:1, :]
        conv = conv + gsr[halo - 1:halo - 1 + tm, :] * cw[1:2, :]
        conv = conv + gsr[halo:halo + tm, :] * cw[2:3, :]
        tail_ref[j - 1] = gsr[tm:tm + halo, :]
        act_ref[...] = (jax.nn.silu(conv) * us_ref[slot][...]).astype(BF16)

    def down():
        o_ref[...] += jnp.dot(act_ref[...], wd_ref[...], preferred_element_type=F32)

    @pl.when(j == 0)
    def _():
        hb_ref[...] = _adaln(x_ref[...], g_ref[...], shift_ref[...], scale_ref[...]).astype(BF16)
        o_ref[...] = jnp.zeros(o_ref.shape, F32)
        up_g(0)
        up_u(0)

    for par in range(2):
        @pl.when((j > 0) & (j < nf) & (lax.rem(j, 2) == par))
        def _():
            fill_halo(1 - par)
            up_g(par)
            gated(1 - par)
            down()
            up_u(par)

    @pl.when(j == nf)
    def _():
        slot = (nf - 1) % 2
        fill_halo(slot)
        gated(slot)
        down()
        res = x_ref[...] + gate_ref[...] * o_ref[...]
        if final_norm:
            ms = jnp.mean(res * res, axis=-1, keepdims=True)
            res = res * lax.rsqrt(ms + NORM_EPS) * fg_ref[...]
        o_ref[...] = res


def _ffn(l, x, mod, g, final_g, w_up, conv_ffn, w_down, final_norm):
    b, s, d = x.shape
    ff = w_down.shape[1]
    tm, tf = _ffn_tiles(s, ff)
    nf = ff // tf
    row = lambda i, t, j: (i, t, 0)
    mod_spec = lambda k: pl.BlockSpec((None, None, None, 1, d), lambda i, t, j: (l, i, k, 0, 0))
    up_tile = lambda j: jnp.minimum(j, nf - 1)
    down_tile = lambda j: jnp.maximum(j - 1, 0)
    return pl.pallas_call(
        functools.partial(_ffn_kernel, nf=nf, final_norm=final_norm),
        grid=(b, s // tm, nf + 1),
        in_specs=[
            pl.BlockSpec((None, tm, d), row),
            mod_spec(3), mod_spec(4), mod_spec(5),
            pl.BlockSpec((None, 1, d), lambda i, t, j: (l, 0, 0)),
            pl.BlockSpec((1, d), lambda i, t, j: (0, 0)),
            pl.BlockSpec((None, None, d, tf), lambda i, t, j: (l, up_tile(j), 0, 0)),
            pl.BlockSpec((None, None, d, tf), lambda i, t, j: (l, nf + up_tile(j), 0, 0)),
            pl.BlockSpec((None, FFN_CONV, tf), lambda i, t, j: (l, 0, down_tile(j))),
            pl.BlockSpec((None, tf, d), lambda i, t, j: (l, down_tile(j), 0)),
        ],
        out_specs=pl.BlockSpec((None, tm, d), row),
        out_shape=jax.ShapeDtypeStruct((b, s, d), F32),
        scratch_shapes=[pltpu.VMEM((tm, d), BF16),
                        pltpu.VMEM((tm + SUBLANES, tf), F32),
                        pltpu.VMEM((tm + SUBLANES, tf), F32),
                        pltpu.VMEM((tm, tf), F32),
                        pltpu.VMEM((tm, tf), F32),
                        pltpu.VMEM((tm, tf), BF16),
                        pltpu.VMEM((nf, SUBLANES, tf), F32)],
        compiler_params=_params("arbitrary", "arbitrary", "arbitrary"),
        name="convglu_ffn",
    )(x, mod, mod, mod, g, final_g, w_up, w_up, conv_ffn, w_down)


def _pack_w_in(w_in):
    nh = MLSTM_HEADS
    o_gates = QK_W + 2 * V_W
    o_attn = o_gates + 2 * nh
    gates = w_in[:, :, o_gates:o_attn]
    gates = jnp.pad(gates, ((0, 0), (0, 0), (0, LANES - 2 * nh)))
    packed = jnp.concatenate([w_in[:, :, :o_gates], w_in[:, :, o_attn:], gates], axis=-1)
    return packed.astype(BF16)


def _sink_rows(sinks):
    depth = sinks.shape[0]
    t = sinks.reshape(depth, ATTN_KV_HEADS, PAIRS, 2).transpose(0, 1, 3, 2)
    t = jnp.repeat(t[..., None], WINDOW, axis=-1)
    return t.reshape(depth, 2 * ATTN_KV_HEADS, PAIRS * WINDOW)


def kernel(x, c, positions, ada_w, ada_b, norm_mix_g, w_in, b_gates, conv_qk, mlstm_head_g, sinks,
           w_out, norm_ffn_g, w_up, conv_ffn, w_down, final_g):
    b, s, d = x.shape
    depth = ada_w.shape[0]
    nh = MLSTM_HEADS

    c_pad = jnp.pad(c, ((0, 2 * SUBLANES - b), (0, 0)))
    mod = _modulation(c_pad, ada_w, ada_b)[:, :b].reshape(depth, b, 6, 1, d)
    cos, sina, sinb = _rope_tables(positions)

    w_in_p = _pack_w_in(w_in)
    w_out_b = w_out.astype(BF16)
    ff = w_down.shape[1]
    _, tf = _ffn_tiles(s, ff)
    w_up_b = w_up.reshape(depth, d, 2 * ff // tf, tf).transpose(0, 2, 1, 3).astype(BF16)
    w_down_b = w_down.astype(BF16)
    bg = jnp.pad(b_gates, ((0, 0), (0, LANES - 2 * nh))).reshape(depth, 1, LANES)
    norm_mix = norm_mix_g.reshape(depth, 1, d)
    norm_ffn = norm_ffn_g.reshape(depth, 1, d)
    head_g = mlstm_head_g.reshape(depth, 1, V_W)
    sink_rows = _sink_rows(sinks)
    final_g2 = final_g.reshape(1, d)

    for l in range(depth):
        qk, v, o_sig, cols, qa, ka, va = _inproj(
            l, x, mod, norm_mix, w_in_p, conv_qk, bg, cos, sina, sinb)
        ym = _mlstm(l, qk, v, o_sig, cols, head_g)
        ya = _attention(l, qa, ka, va, sink_rows)
        x = _outproj(l, ym, ya, x, mod, w_out_b)
        x = _ffn(l, x, mod, norm_ffn, final_g2, w_up_b, conv_ffn, w_down_b,
                 final_norm=(l == depth - 1))
    return x
```

```python
import functools

import jax
import jax.numpy as jnp
from jax import lax
from jax.experimental import pallas as pl
from jax.experimental.pallas import tpu as pltpu

F32 = jnp.float32
BF16 = jnp.bfloat16

MLSTM_HEADS = 4
MLSTM_DQK = 128
MLSTM_DV = 256
MLSTM_CONV = 4
MLSTM_CHUNK = 256
GATE_SOFTCAP = 15.0
ATTN_HEAD_DIM = 64
ATTN_HEADS = 16
ATTN_KV_HEADS = 2
WINDOW = 128
ROPE_DIM = 16
ROPE_THETA = 500000.0
FFN_CONV = 3
NORM_EPS = 1e-6

LANES = 128
SUBLANES = 8
VMEM_LIMIT = 52 * 1024 * 1024

QK_W = 2 * MLSTM_HEADS * MLSTM_DQK
V_W = MLSTM_HEADS * MLSTM_DV
QA_W = ATTN_HEADS * ATTN_HEAD_DIM
KV_W = ATTN_KV_HEADS * ATTN_HEAD_DIM
OFF_QK = 0
OFF_V = OFF_QK + QK_W
OFF_O = OFF_V + V_W
OFF_QA = OFF_O + V_W
OFF_KA = OFF_QA + QA_W
OFF_VA = OFF_KA + KV_W
OFF_G = OFF_VA + KV_W
IN_PACKED = OFF_G + LANES

COL_G = 0
COL_B = MLSTM_HEADS
COL_CG = 2 * MLSTM_HEADS
PAIRS = ATTN_HEADS // ATTN_KV_HEADS // 2


def _params(*sem):
    return pltpu.CompilerParams(dimension_semantics=sem, vmem_limit_bytes=VMEM_LIMIT)


def _pick_tile(n, pref):
    t = min(n, pref)
    while n % t:
        t //= 2
    return t


def _ffn_tiles(s, ff):
    return _pick_tile(s, 1024), _pick_tile(ff, 256)


def _mod_kernel(c_ref, w_ref, b_ref, o_ref):
    ca = jax.nn.silu(c_ref[...]).astype(BF16)
    o_ref[...] = jnp.dot(ca, w_ref[...].astype(BF16), preferred_element_type=F32) + b_ref[...]


def _modulation(c_pad, ada_w, ada_b):
    depth, d, n = ada_w.shape
    rows = c_pad.shape[0]
    tn = _pick_tile(n, 1024)
    return pl.pallas_call(
        _mod_kernel,
        grid=(depth, n // tn),
        in_specs=[
            pl.BlockSpec((rows, d), lambda l, j: (0, 0)),
            pl.BlockSpec((None, d, tn), lambda l, j: (l, 0, j)),
            pl.BlockSpec((None, 1, tn), lambda l, j: (l, 0, j)),
        ],
        out_specs=pl.BlockSpec((None, rows, tn), lambda l, j: (l, 0, j)),
        out_shape=jax.ShapeDtypeStruct((depth, rows, n), F32),
        compiler_params=_params("arbitrary", "arbitrary"),
        name="adaln_mod",
    )(c_pad, ada_w, ada_b.reshape(depth, 1, n))


def _rope_kernel(pos_ref, freq_ref, cos_ref, sina_ref, sinb_ref):
    ang = pos_ref[...].astype(F32) * freq_ref[...]
    d = lax.broadcasted_iota(jnp.int32, ang.shape, 1) % ATTN_HEAD_DIM
    c = jnp.cos(ang)
    s = jnp.sin(ang)
    half = ROPE_DIM // 2
    cos_ref[...] = jnp.where(d < ROPE_DIM, c, 1.0)
    sina_ref[...] = jnp.where(d < half, -s, 0.0)
    sinb_ref[...] = jnp.where((d >= half) & (d < ROPE_DIM), s, 0.0)


def _rope_tables(positions):
    b, s = positions.shape
    ts = _pick_tile(s, 1024)
    inv_freq = ROPE_THETA ** (-jnp.arange(0, ROPE_DIM, 2, dtype=F32) / ROPE_DIM)
    per_head = jnp.concatenate(
        [inv_freq, inv_freq, jnp.zeros((ATTN_HEAD_DIM - ROPE_DIM,), F32)])
    freq = jnp.tile(per_head, LANES // ATTN_HEAD_DIM).reshape(1, LANES)
    tab = jax.ShapeDtypeStruct((b, s, LANES), F32)
    spec = pl.BlockSpec((None, ts, LANES), lambda i, j: (i, j, 0))
    return pl.pallas_call(
        _rope_kernel,
        grid=(b, s // ts),
        in_specs=[pl.BlockSpec((None, ts, 1), lambda i, j: (i, j, 0)),
                  pl.BlockSpec((1, LANES), lambda i, j: (0, 0))],
        out_specs=[spec, spec, spec],
        out_shape=[tab, tab, tab],
        compiler_params=_params("arbitrary", "arbitrary"),
        name="rope_tables",
    )(positions.reshape(b, s, 1), freq)


def _adaln(x, g, shift, scale):
    ms = jnp.mean(x * x, axis=-1, keepdims=True)
    h = x * lax.rsqrt(ms + NORM_EPS) * g
    return h * (1.0 + scale) + shift


def _rope(t, cos, sina, sinb):
    n = t.shape[-1]
    half = ROPE_DIM // 2
    return t * cos + pltpu.roll(t, n - half, 1) * sina + pltpu.roll(t, half, 1) * sinb


def _chunk_scan_rows(x, op, fill):
    pos = lax.broadcasted_iota(jnp.int32, x.shape, 0) % MLSTM_CHUNK
    step = 1
    while step < MLSTM_CHUNK:
        x = op(x, jnp.where(pos >= step, pltpu.roll(x, step, 0), fill))
        step *= 2
    return x


def _gate_columns(gates):
    nh = MLSTM_HEADS
    lane = lax.broadcasted_iota(jnp.int32, gates.shape, 1)
    b = _chunk_scan_rows(jax.nn.log_sigmoid(gates), jnp.add, 0.0)
    g = gates - pltpu.roll(b, LANES - nh, 1)
    cg = _chunk_scan_rows(g, jnp.maximum, -jnp.inf)
    return jnp.where(lane < COL_B, g, jnp.where(lane < COL_CG, b, pltpu.roll(cg, COL_CG, 1)))


def _inproj_kernel(x_ref, shift_ref, scale_ref, g_ref, w_ref, cw_ref, bg_ref,
                   cos_ref, sina_ref, sinb_ref,
                   qk_ref, v_ref, o_ref, col_ref, qa_ref, ka_ref, va_ref,
                   hb_ref, cbuf_ref):
    tm = x_ref.shape[0]
    halo = SUBLANES

    @pl.when(pl.program_id(1) == 0)
    def _():
        cbuf_ref[...] = jnp.zeros(cbuf_ref.shape, F32)

    hb_ref[...] = _adaln(x_ref[...], g_ref[...], shift_ref[...], scale_ref[...]).astype(BF16)

    def proj(off, width):
        return jnp.dot(hb_ref[...], w_ref[:, off:off + width], preferred_element_type=F32)

    pg = proj(OFF_G, LANES) + bg_ref[...]
    col_ref[...] = _gate_columns(GATE_SOFTCAP * jnp.tanh(pg / GATE_SOFTCAP))

    cw = cw_ref[...]
    half = QK_W // 2
    for c in range(2):
        lo = c * half
        pq = proj(OFF_QK + lo, half)
        prev = cbuf_ref[:, lo:lo + half]
        acc = _shift_rows(pq, prev, 3) * cw[0:1, lo:lo + half]
        acc = acc + _shift_rows(pq, prev, 2) * cw[1:2, lo:lo + half]
        acc = acc + _shift_rows(pq, prev, 1) * cw[2:3, lo:lo + half]
        acc = acc + pq * cw[3:4, lo:lo + half]
        cbuf_ref[:, lo:lo + half] = pq[tm - halo:tm, :]
        act = jax.nn.silu(acc)
        if c == 0:
            act = act * (MLSTM_DQK ** -0.5)
        qk_ref[:, lo:lo + half] = act.astype(BF16)
        vlo = c * (V_W // 2)
        v_ref[:, vlo:vlo + V_W // 2] = proj(OFF_V + vlo, V_W // 2).astype(BF16)

    for c in range(2):
        lo = c * (V_W // 2)
        o_ref[:, lo:lo + V_W // 2] = jax.nn.sigmoid(proj(OFF_O + lo, V_W // 2))

    cos = cos_ref[...]
    sina = sina_ref[...]
    sinb = sinb_ref[...]
    for c in range(QA_W // 256):
        pa = proj(OFF_QA + c * 256, 256)
        for k in range(2):
            lo = c * 256 + k * LANES
            r = _rope(pa[:, k * LANES:(k + 1) * LANES], cos, sina, sinb)
            qa_ref[:, lo:lo + LANES] = (r * (ATTN_HEAD_DIM ** -0.5)).astype(BF16)

    pkv = proj(OFF_KA, 2 * KV_W)
    ka_ref[...] = _rope(pkv[:, 0:KV_W], cos, sina, sinb)
    va_ref[...] = pkv[:, KV_W:2 * KV_W]


def _inproj(l, x, mod, g, w_packed, conv_qk, bg, cos, sina, sinb):
    b, s, d = x.shape
    tm = _pick_tile(s, 512)
    row = lambda i, j: (i, j, 0)
    lay = lambda i, j: (l, 0, 0)
    outs = [
        (QK_W, BF16), (V_W, BF16), (V_W, F32), (LANES, F32), (QA_W, BF16), (KV_W, F32), (KV_W, F32),
    ]
    return pl.pallas_call(
        _inproj_kernel,
        grid=(b, s // tm),
        in_specs=[
            pl.BlockSpec((None, tm, d), row),
            pl.BlockSpec((None, None, None, 1, d), lambda i, j: (l, i, 0, 0, 0)),
            pl.BlockSpec((None, None, None, 1, d), lambda i, j: (l, i, 1, 0, 0)),
            pl.BlockSpec((None, 1, d), lay),
            pl.BlockSpec((None, d, IN_PACKED), lay),
            pl.BlockSpec((None, MLSTM_CONV, QK_W), lay),
            pl.BlockSpec((None, 1, LANES), lay),
            pl.BlockSpec((None, tm, LANES), row),
            pl.BlockSpec((None, tm, LANES), row),
            pl.BlockSpec((None, tm, LANES), row),
        ],
        out_specs=[pl.BlockSpec((None, tm, w), row) for w, _ in outs],
        out_shape=[jax.ShapeDtypeStruct((b, s, w), dt) for w, dt in outs],
        scratch_shapes=[pltpu.VMEM((tm, d), BF16), pltpu.VMEM((SUBLANES, QK_W), F32)],
        compiler_params=_params("arbitrary", "arbitrary"),
        name="in_proj",
    )(x, mod, mod, g, w_packed, conv_qk, bg, cos, sina, sinb)


def _gate_selector():
    nh = MLSTM_HEADS
    src = jnp.arange(LANES)[:, None]
    group = jnp.arange(3 * nh * LANES)[None, :] // LANES
    base = jnp.asarray([COL_CG, COL_G, COL_B])[group % 3]
    return (src == base + group // 3).astype(BF16)


def _mlstm_kernel(qk_ref, v_ref, o_ref, col_ref, hg_ref, sel_ref, y_ref, c_ref, m_ref):
    L = MLSTM_CHUNK
    nh = MLSTM_HEADS
    n_chunks = qk_ref.shape[0] // L

    @pl.when(pl.program_id(1) == 0)
    def _():
        c_ref[...] = jnp.zeros(c_ref.shape, F32)
        m_ref[...] = jnp.zeros(m_ref.shape, F32)

    ri = lax.broadcasted_iota(jnp.int32, (L, L), 0)
    ci = lax.broadcasted_iota(jnp.int32, (L, L), 1)
    causal = ci <= ri
    ones_col = (lax.broadcasted_iota(jnp.int32, (L, LANES), 1) == 0).astype(BF16)
    nt = (((1,), (1,)), ((), ()))
    tn = (((0,), (0,)), ((), ()))
    units = [(c, h) for c in range(n_chunks) for h in range(nh)]
    rows = [slice(c * L, (c + 1) * L) for c in range(n_chunks)]
    dvs = [slice(h * MLSTM_DV, (h + 1) * MLSTM_DV) for h in range(nh)]

    col = [col_ref[r, :] for r in rows]
    col_t = [cl.T for cl in col]
    q = {(c, h): qk_ref[rows[c], h * MLSTM_DQK:(h + 1) * MLSTM_DQK] for c, h in units}
    k = {(c, h): qk_ref[rows[c], (nh + h) * MLSTM_DQK:(nh + h + 1) * MLSTM_DQK] for c, h in units}
    v_ext = {(c, h): jnp.concatenate([v_ref[rows[c], dvs[h]], ones_col], axis=1) for c, h in units}

    def replicate(x):
        hi = x.astype(BF16)
        rem = x - hi.astype(F32)
        mid = rem.astype(BF16)
        lo = (rem - mid.astype(F32)).astype(BF16)
        sel = sel_ref[...]
        return ((jnp.dot(hi, sel, preferred_element_type=F32)
                 + jnp.dot(mid, sel, preferred_element_type=F32))
                + jnp.dot(lo, sel, preferred_element_type=F32))

    def twice(x):
        return jnp.concatenate([x, x], axis=1)

    rep = [replicate(cl) for cl in col]

    def gate(c, h, kind):
        lo = (3 * h + kind) * LANES
        return rep[c][:, lo:lo + LANES]

    mm, w_inter, decay, w_state = {}, {}, {}, {}
    for h in range(nh):
        m = m_ref[h:h + 1, :]
        for c in range(n_chunks):
            mm[c, h] = jnp.maximum(m, gate(c, h, 0))
            mm_last = mm[c, h][L - 1:L, :]
            w_inter[c, h] = jnp.exp(m - mm[c, h])
            decay[c, h] = jnp.exp(m - mm_last)
            w_state[c, h] = jnp.exp(gate(c, h, 1) - mm_last)
            m = gate(c, h, 2)[L - 1:L, :] + mm_last
        m_ref[h:h + 1, :] = m

    s = {}
    for u in units:
        c, h = u
        g_row = col_t[c][COL_G + h:COL_G + h + 1, :]
        w_intra = jnp.exp(jnp.where(causal, g_row - twice(mm[u]), -jnp.inf))
        s[u] = lax.dot_general(q[u], k[u], nt, preferred_element_type=F32) * w_intra
    intra = {u: jnp.dot(s[u].astype(BF16), v_ext[u][:, 0:MLSTM_DV], preferred_element_type=F32)
             for u in units}
    upd = {u: lax.dot_general((k[u].astype(F32) * w_state[u]).astype(BF16), v_ext[u], tn,
                              preferred_element_type=F32) for u in units}

    c_in = {}
    for h in range(nh):
        state = c_ref[h]
        for c in range(n_chunks):
            c_in[c, h] = state.astype(BF16)
            d = decay[c, h]
            state = jnp.concatenate([d, d, d], axis=1) * state + upd[c, h]
        c_ref[h] = state

    def lane_sum(x):
        return jnp.sum(x[:, 0:LANES] + x[:, LANES:2 * LANES], axis=-1, keepdims=True)

    for u in units:
        c, h = u
        inter = jnp.dot(q[u], c_in[u], preferred_element_type=F32)
        num = intra[u] + twice(w_inter[u]) * inter[:, 0:MLSTM_DV]
        nq = lane_sum(s[u]) + w_inter[u][:, 0:1] * inter[:, MLSTM_DV:MLSTM_DV + 1]
        m_t = gate(c, h, 2)[:, 0:1] + mm[u][:, 0:1]
        rden = 1.0 / jnp.maximum(jnp.abs(nq), jnp.exp(-m_t))
        msq = lane_sum(num * num) * (1.0 / MLSTM_DV)
        scale = rden * lax.rsqrt(rden * rden * msq + NORM_EPS)
        hn = num * scale * hg_ref[:, dvs[h]]
        y_ref[rows[c], dvs[h]] = (o_ref[rows[c], dvs[h]] * hn).astype(BF16)


def _mlstm(l, qk, v, o_sig, cols, head_g, sel):
    b, s, _ = qk.shape
    t = MLSTM_CHUNK
    assert s % t == 0, (s, t)
    row = lambda i, j: (i, j, 0)
    return pl.pallas_call(
        _mlstm_kernel,
        grid=(b, s // t),
        in_specs=[
            pl.BlockSpec((None, t, QK_W), row),
            pl.BlockSpec((None, t, V_W), row),
            pl.BlockSpec((None, t, V_W), row),
            pl.BlockSpec((None, t, LANES), row),
            pl.BlockSpec((None, 1, V_W), lambda i, j: (l, 0, 0)),
            pl.BlockSpec(sel.shape, lambda i, j: (0, 0)),
        ],
        out_specs=pl.BlockSpec((None, t, V_W), row),
        out_shape=jax.ShapeDtypeStruct((b, s, V_W), BF16),
        scratch_shapes=[pltpu.VMEM((MLSTM_HEADS, MLSTM_DQK, MLSTM_DV + LANES), F32),
                        pltpu.VMEM((SUBLANES, LANES), F32)],
        compiler_params=_params("arbitrary", "arbitrary"),
        name="mlstm",
    )(qk, v, o_sig, cols, head_g, sel)


def _block_diag(band, kv_head):
    lane = lax.broadcasted_iota(jnp.int32, band.shape, 1)
    low = lane < ATTN_HEAD_DIM
    rolled = pltpu.roll(band, ATTN_HEAD_DIM, 1)
    if kv_head == 0:
        top = jnp.where(low, band, 0.0)
        bot = jnp.where(low, 0.0, rolled)
    else:
        top = jnp.where(low, rolled, 0.0)
        bot = jnp.where(low, 0.0, band)
    return jnp.concatenate([top, bot], axis=0).astype(BF16)


def _attn_kernel(q_ref, kc_ref, kp_ref, vc_ref, vp_ref, sink_ref, y_ref):
    W = WINDOW
    n_blocks = q_ref.shape[0] // W
    prev_penalty = jnp.where(pl.program_id(1) == 0, -jnp.inf, 0.0)

    key = lax.broadcasted_iota(jnp.int32, (W, PAIRS * W), 0)
    qry = lax.broadcasted_iota(jnp.int32, (W, PAIRS * W), 1) % W
    use_prev = key > qry

    def softmax_t(prev, cur, sink):
        comb = jnp.where(use_prev, prev, cur)
        m = jnp.maximum(jnp.max(comb, axis=0, keepdims=True), sink)
        p = jnp.exp(comb - m)
        den = jnp.sum(p, axis=0, keepdims=True) + jnp.exp(sink - m)
        p = p * (1.0 / den)
        return jnp.where(use_prev, p, 0.0), jnp.where(use_prev, 0.0, p)

    for blk in range(n_blocks):
        rows = slice(blk * W, (blk + 1) * W)
        if blk == 0:
            k_prev, v_prev = kp_ref[...], vp_ref[...]
        else:
            prev = slice((blk - 1) * W, blk * W)
            k_prev, v_prev = kc_ref[prev, :], vc_ref[prev, :]
        k_band = jnp.concatenate([k_prev, kc_ref[rows, :]], axis=0)
        v_band = jnp.concatenate([v_prev, vc_ref[rows, :]], axis=0)
        for kvh in range(ATTN_KV_HEADS):
            k_bd = _block_diag(k_band, kvh)
            v_bd = _block_diag(v_band, kvh)
            lanes = [slice((kvh * PAIRS + p) * LANES, (kvh * PAIRS + p + 1) * LANES)
                     for p in range(PAIRS)]
            q_all = jnp.concatenate([q_ref[rows, ln] for ln in lanes], axis=0)
            st = lax.dot_general(k_bd, q_all, (((1,), (1,)), ((), ())),
                                 preferred_element_type=F32)
            prev_e, cur_e, prev_o, cur_o = (st[i * W:(i + 1) * W, :] for i in range(4))
            if blk == 0:
                prev_e = prev_e + prev_penalty
                prev_o = prev_o + prev_penalty
            pe_prev, pe_cur = softmax_t(prev_e, cur_e, sink_ref[2 * kvh:2 * kvh + 1, :])
            po_prev, po_cur = softmax_t(prev_o, cur_o, sink_ref[2 * kvh + 1:2 * kvh + 2, :])
            pt = jnp.concatenate([pe_prev, pe_cur, po_prev, po_cur], axis=0).astype(BF16)
            out_t = lax.dot_general(v_bd, pt, (((0,), (0,)), ((), ())),
                                    preferred_element_type=F32)
            for p in range(PAIRS):
                y_ref[rows, lanes[p]] = out_t[:, p * W:(p + 1) * W].T.astype(BF16)


def _attention(l, qa, ka, va, sink_rows):
    b, s, _ = qa.shape
    tq = _pick_tile(s, 512)
    per = tq // WINDOW
    cur = lambda i, j: (i, j, 0)
    prev = lambda i, j: (i, jnp.maximum(j * per - 1, 0), 0)
    return pl.pallas_call(
        _attn_kernel,
        grid=(b, s // tq),
        in_specs=[
            pl.BlockSpec((None, tq, QA_W), cur),
            pl.BlockSpec((None, tq, KV_W), cur),
            pl.BlockSpec((None, WINDOW, KV_W), prev),
            pl.BlockSpec((None, tq, KV_W), cur),
            pl.BlockSpec((None, WINDOW, KV_W), prev),
            pl.BlockSpec((None, 2 * ATTN_KV_HEADS, PAIRS * WINDOW), lambda i, j: (l, 0, 0)),
        ],
        out_specs=pl.BlockSpec((None, tq, QA_W), cur),
        out_shape=jax.ShapeDtypeStruct((b, s, QA_W), BF16),
        compiler_params=_params("arbitrary", "arbitrary"),
        name="swa_attention",
    )(qa, ka, ka, va, va, sink_rows)


def _outproj_kernel(ym_ref, ya_ref, x_ref, gate_ref, w_ref, o_ref):
    km = ym_ref.shape[1]
    y = jnp.dot(ym_ref[...], w_ref[0:km, :], preferred_element_type=F32)
    y = y + jnp.dot(ya_ref[...], w_ref[km:, :], preferred_element_type=F32)
    o_ref[...] = x_ref[...] + gate_ref[...] * y


def _outproj(l, ym, ya, x, mod, w_out):
    b, s, d = x.shape
    tm = _pick_tile(s, 512)
    row = lambda i, j: (i, j, 0)
    return pl.pallas_call(
        _outproj_kernel,
        grid=(b, s // tm),
        in_specs=[
            pl.BlockSpec((None, tm, ym.shape[2]), row),
            pl.BlockSpec((None, tm, ya.shape[2]), row),
            pl.BlockSpec((None, tm, d), row),
            pl.BlockSpec((None, None, None, 1, d), lambda i, j: (l, i, 2, 0, 0)),
            pl.BlockSpec((None,) + w_out.shape[1:], lambda i, j: (l, 0, 0)),
        ],
        out_specs=pl.BlockSpec((None, tm, d), row),
        out_shape=jax.ShapeDtypeStruct((b, s, d), F32),
        compiler_params=_params("arbitrary", "arbitrary"),
        name="out_proj",
    )(ym, ya, x, mod, w_out)


def _shift_rows(x, prev_tail, k):
    rolled = pltpu.roll(x, k, 0)
    sub = lax.broadcasted_iota(jnp.int32, prev_tail.shape, 0)
    head = jnp.where(sub < k, pltpu.roll(prev_tail, k, 0), rolled[0:SUBLANES, :])
    return jnp.concatenate([head, rolled[SUBLANES:, :]], axis=0)


def _ffn_kernel(x_ref, shift_ref, scale_ref, gate_ref, g_ref, fg_ref, wg_ref, wu_ref, cw_ref, wd_ref,
                o_ref, hb_ref, gs0_ref, gs1_ref, us0_ref, us1_ref, act_ref, halo_ref, tail_ref, *,
                nf, final_norm):
    gs_ref = (gs0_ref, gs1_ref)
    us_ref = (us0_ref, us1_ref)
    tm = x_ref.shape[0]
    halo = SUBLANES
    i = pl.program_id(1)
    j = pl.program_id(2)

    def up_g(slot):
        gs_ref[slot][...] = jnp.dot(hb_ref[...], wg_ref[...], preferred_element_type=F32)

    def up_u(slot):
        us_ref[slot][...] = jnp.dot(hb_ref[...], wu_ref[...], preferred_element_type=F32)

    def fill_halo():
        @pl.when(i == 0)
        def _():
            halo_ref[...] = jnp.zeros(halo_ref.shape, F32)

        @pl.when(i > 0)
        def _():
            halo_ref[...] = tail_ref[j - 1]

    def gated(slot):
        cw = cw_ref[...]
        g = gs_ref[slot][...]
        prev = halo_ref[...]
        conv = _shift_rows(g, prev, 2) * cw[0:1, :]
        conv = conv + _shift_rows(g, prev, 1) * cw[1:2, :]
        conv = conv + g * cw[2:3, :]
        tail_ref[j - 1] = g[tm - halo:tm, :]
        act_ref[...] = (jax.nn.silu(conv) * us_ref[slot][...]).astype(BF16)

    def down():
        o_ref[...] += jnp.dot(act_ref[...], wd_ref[...], preferred_element_type=F32)

    @pl.when(j == 0)
    def _():
        hb_ref[...] = _adaln(x_ref[...], g_ref[...], shift_ref[...], scale_ref[...]).astype(BF16)
        o_ref[...] = jnp.zeros(o_ref.shape, F32)
        up_g(0)
        up_u(0)

    for par in range(2):
        @pl.when((j > 0) & (j < nf) & (lax.rem(j, 2) == par))
        def _():
            fill_halo()
            up_g(par)
            gated(1 - par)
            down()
            up_u(par)

    @pl.when(j == nf)
    def _():
        fill_halo()
        gated((nf - 1) % 2)
        down()
        res = x_ref[...] + gate_ref[...] * o_ref[...]
        if final_norm:
            ms = jnp.mean(res * res, axis=-1, keepdims=True)
            res = res * lax.rsqrt(ms + NORM_EPS) * fg_ref[...]
        o_ref[...] = res


def _ffn(l, x, mod, g, final_g, w_up, conv_ffn, w_down, final_norm):
    b, s, d = x.shape
    ff = w_down.shape[1]
    tm, tf = _ffn_tiles(s, ff)
    nf = ff // tf
    row = lambda i, t, j: (i, t, 0)
    mod_spec = lambda k: pl.BlockSpec((None, None, None, 1, d), lambda i, t, j: (l, i, k, 0, 0))
    up_tile = lambda j: jnp.minimum(j, nf - 1)
    down_tile = lambda j: jnp.maximum(j - 1, 0)
    return pl.pallas_call(
        functools.partial(_ffn_kernel, nf=nf, final_norm=final_norm),
        grid=(b, s // tm, nf + 1),
        in_specs=[
            pl.BlockSpec((None, tm, d), row),
            mod_spec(3), mod_spec(4), mod_spec(5),
            pl.BlockSpec((None, 1, d), lambda i, t, j: (l, 0, 0)),
            pl.BlockSpec((1, d), lambda i, t, j: (0, 0)),
            pl.BlockSpec((None, d, tf), lambda i, t, j: (l, 0, up_tile(j))),
            pl.BlockSpec((None, d, tf), lambda i, t, j: (l, 0, nf + up_tile(j))),
            pl.BlockSpec((None, FFN_CONV, tf), lambda i, t, j: (l, 0, down_tile(j))),
            pl.BlockSpec((None, tf, d), lambda i, t, j: (l, down_tile(j), 0)),
        ],
        out_specs=pl.BlockSpec((None, tm, d), row),
        out_shape=jax.ShapeDtypeStruct((b, s, d), F32),
        scratch_shapes=[pltpu.VMEM((tm, d), BF16),
                        pltpu.VMEM((tm, tf), F32),
                        pltpu.VMEM((tm, tf), F32),
                        pltpu.VMEM((tm, tf), F32),
                        pltpu.VMEM((tm, tf), F32),
                        pltpu.VMEM((tm, tf), BF16),
                        pltpu.VMEM((SUBLANES, tf), F32),
                        pltpu.VMEM((nf, SUBLANES, tf), F32)],
        compiler_params=_params("arbitrary", "arbitrary", "arbitrary"),
        name="convglu_ffn",
    )(x, mod, mod, mod, g, final_g, w_up, w_up, conv_ffn, w_down)


def _pack_w_in(w_in):
    nh = MLSTM_HEADS
    o_gates = QK_W + 2 * V_W
    o_attn = o_gates + 2 * nh
    gates = w_in[:, :, o_gates:o_attn]
    gates = jnp.pad(gates, ((0, 0), (0, 0), (0, LANES - 2 * nh)))
    packed = jnp.concatenate([w_in[:, :, :o_gates], w_in[:, :, o_attn:], gates], axis=-1)
    return packed.astype(BF16)


def _sink_rows(sinks):
    depth = sinks.shape[0]
    t = sinks.reshape(depth, ATTN_KV_HEADS, PAIRS, 2).transpose(0, 1, 3, 2)
    t = jnp.repeat(t[..., None], WINDOW, axis=-1)
    return t.reshape(depth, 2 * ATTN_KV_HEADS, PAIRS * WINDOW)


def kernel(x, c, positions, ada_w, ada_b, norm_mix_g, w_in, b_gates, conv_qk, mlstm_head_g, sinks,
           w_out, norm_ffn_g, w_up, conv_ffn, w_down, final_g):
    b, s, d = x.shape
    depth = ada_w.shape[0]
    nh = MLSTM_HEADS

    c_pad = jnp.pad(c, ((0, 2 * SUBLANES - b), (0, 0)))
    mod = _modulation(c_pad, ada_w, ada_b)[:, :b].reshape(depth, b, 6, 1, d)
    cos, sina, sinb = _rope_tables(positions)

    w_in_p = _pack_w_in(w_in)
    w_out_b = w_out.astype(BF16)
    w_up_b = w_up.astype(BF16)
    w_down_b = w_down.astype(BF16)
    bg = jnp.pad(b_gates, ((0, 0), (0, LANES - 2 * nh))).reshape(depth, 1, LANES)
    norm_mix = norm_mix_g.reshape(depth, 1, d)
    norm_ffn = norm_ffn_g.reshape(depth, 1, d)
    head_g = mlstm_head_g.reshape(depth, 1, V_W)
    sink_rows = _sink_rows(sinks)
    gate_sel = _gate_selector()
    final_g2 = final_g.reshape(1, d)

    for l in range(depth):
        qk, v, o_sig, cols, qa, ka, va = _inproj(
            l, x, mod, norm_mix, w_in_p, conv_qk, bg, cos, sina, sinb)
        ym = _mlstm(l, qk, v, o_sig, cols, head_g, gate_sel)
        ya = _attention(l, qa, ka, va, sink_rows)
        x = _outproj(l, ym, ya, x, mod, w_out_b)
        x = _ffn(l, x, mod, norm_ffn, final_g2, w_up_b, conv_ffn, w_down_b,
                 final_norm=(l == depth - 1))
    return x
```

```python
import functools

import jax
import jax.numpy as jnp
from jax import lax
from jax.experimental import pallas as pl
from jax.experimental.pallas import tpu as pltpu

F32 = jnp.float32
BF16 = jnp.bfloat16

MLSTM_HEADS = 4
MLSTM_DQK = 128
MLSTM_DV = 256
MLSTM_CONV = 4
MLSTM_CHUNK = 256
GATE_SOFTCAP = 15.0
ATTN_HEAD_DIM = 64
ATTN_HEADS = 16
ATTN_KV_HEADS = 2
WINDOW = 128
ROPE_DIM = 16
ROPE_THETA = 500000.0
FFN_CONV = 3
NORM_EPS = 1e-6

LANES = 128
SUBLANES = 8
VMEM_LIMIT = 61 * 1024 * 1024

QK_W = 2 * MLSTM_HEADS * MLSTM_DQK
V_W = MLSTM_HEADS * MLSTM_DV
QA_W = ATTN_HEADS * ATTN_HEAD_DIM
KV_W = ATTN_KV_HEADS * ATTN_HEAD_DIM
OFF_QK = 0
OFF_V = OFF_QK + QK_W
OFF_O = OFF_V + V_W
OFF_QA = OFF_O + V_W
OFF_KA = OFF_QA + QA_W
OFF_VA = OFF_KA + KV_W
OFF_G = OFF_VA + KV_W
IN_PACKED = OFF_G + LANES

COL_G = 0
COL_B = MLSTM_HEADS
COL_CG = 2 * MLSTM_HEADS
PAIRS = ATTN_HEADS // ATTN_KV_HEADS // 2


def _params(*sem):
    return pltpu.CompilerParams(dimension_semantics=sem, vmem_limit_bytes=VMEM_LIMIT)


def _pick_tile(n, pref):
    t = min(n, pref)
    while n % t:
        t //= 2
    return t


def _ffn_tiles(s, ff):
    return _pick_tile(s, 1024), _pick_tile(ff, 512)


def _mod_kernel(c_ref, w_ref, b_ref, o_ref):
    ca = jax.nn.silu(c_ref[...]).astype(BF16)
    o_ref[...] = jnp.dot(ca, w_ref[...].astype(BF16), preferred_element_type=F32) + b_ref[...]


def _modulation(c_pad, ada_w, ada_b):
    depth, d, n = ada_w.shape
    rows = c_pad.shape[0]
    tn = _pick_tile(n, 1024)
    return pl.pallas_call(
        _mod_kernel,
        grid=(depth, n // tn),
        in_specs=[
            pl.BlockSpec((rows, d), lambda l, j: (0, 0)),
            pl.BlockSpec((None, d, tn), lambda l, j: (l, 0, j)),
            pl.BlockSpec((None, 1, tn), lambda l, j: (l, 0, j)),
        ],
        out_specs=pl.BlockSpec((None, rows, tn), lambda l, j: (l, 0, j)),
        out_shape=jax.ShapeDtypeStruct((depth, rows, n), F32),
        compiler_params=_params("arbitrary", "arbitrary"),
        name="adaln_mod",
    )(c_pad, ada_w, ada_b.reshape(depth, 1, n))


def _rope_kernel(pos_ref, freq_ref, cos_ref, sina_ref, sinb_ref):
    ang = pos_ref[...].astype(F32) * freq_ref[...]
    d = lax.broadcasted_iota(jnp.int32, ang.shape, 1) % ATTN_HEAD_DIM
    c = jnp.cos(ang)
    s = jnp.sin(ang)
    half = ROPE_DIM // 2
    cos_ref[...] = jnp.where(d < ROPE_DIM, c, 1.0)
    sina_ref[...] = jnp.where(d < half, -s, 0.0)
    sinb_ref[...] = jnp.where((d >= half) & (d < ROPE_DIM), s, 0.0)


def _rope_tables(positions):
    b, s = positions.shape
    ts = _pick_tile(s, 1024)
    inv_freq = ROPE_THETA ** (-jnp.arange(0, ROPE_DIM, 2, dtype=F32) / ROPE_DIM)
    per_head = jnp.concatenate(
        [inv_freq, inv_freq, jnp.zeros((ATTN_HEAD_DIM - ROPE_DIM,), F32)])
    freq = jnp.tile(per_head, LANES // ATTN_HEAD_DIM).reshape(1, LANES)
    tab = jax.ShapeDtypeStruct((b, s, LANES), F32)
    spec = pl.BlockSpec((None, ts, LANES), lambda i, j: (i, j, 0))
    return pl.pallas_call(
        _rope_kernel,
        grid=(b, s // ts),
        in_specs=[pl.BlockSpec((None, ts, 1), lambda i, j: (i, j, 0)),
                  pl.BlockSpec((1, LANES), lambda i, j: (0, 0))],
        out_specs=[spec, spec, spec],
        out_shape=[tab, tab, tab],
        compiler_params=_params("arbitrary", "arbitrary"),
        name="rope_tables",
    )(positions.reshape(b, s, 1), freq)


def _adaln(x, g, shift, scale):
    ms = jnp.mean(x * x, axis=-1, keepdims=True)
    h = x * lax.rsqrt(ms + NORM_EPS) * g
    return h * (1.0 + scale) + shift


def _rope(t, cos, sina, sinb):
    n = t.shape[-1]
    half = ROPE_DIM // 2
    return t * cos + pltpu.roll(t, n - half, 1) * sina + pltpu.roll(t, half, 1) * sinb


def _chunk_scan_rows(x, op, fill):
    pos = lax.broadcasted_iota(jnp.int32, x.shape, 0) % MLSTM_CHUNK
    step = 1
    while step < MLSTM_CHUNK:
        x = op(x, jnp.where(pos >= step, pltpu.roll(x, step, 0), fill))
        step *= 2
    return x


def _gate_columns(gates):
    nh = MLSTM_HEADS
    lane = lax.broadcasted_iota(jnp.int32, gates.shape, 1)
    b = _chunk_scan_rows(jax.nn.log_sigmoid(gates), jnp.add, 0.0)
    g = gates - pltpu.roll(b, LANES - nh, 1)
    cg = _chunk_scan_rows(g, jnp.maximum, -jnp.inf)
    return jnp.where(lane < COL_B, g, jnp.where(lane < COL_CG, b, pltpu.roll(cg, COL_CG, 1)))


def _inproj_kernel(x_ref, shift_ref, scale_ref, g_ref, w_ref, cw_ref, bg_ref,
                   cos_ref, sina_ref, sinb_ref,
                   qk_ref, v_ref, o_ref, col_ref, qa_ref, ka_ref, va_ref,
                   hb_ref, cbuf_ref):
    tm = x_ref.shape[0]
    halo = SUBLANES

    @pl.when(pl.program_id(1) == 0)
    def _():
        cbuf_ref[...] = jnp.zeros(cbuf_ref.shape, F32)

    hb_ref[...] = _adaln(x_ref[...], g_ref[...], shift_ref[...], scale_ref[...]).astype(BF16)

    def proj(off, width):
        return jnp.dot(hb_ref[...], w_ref[:, off:off + width], preferred_element_type=F32)

    pg = proj(OFF_G, LANES) + bg_ref[...]
    col_ref[...] = _gate_columns(GATE_SOFTCAP * jnp.tanh(pg / GATE_SOFTCAP))

    cw = cw_ref[...]
    half = QK_W // 2
    for c in range(2):
        lo = c * half
        pq = proj(OFF_QK + lo, half)
        prev = cbuf_ref[:, lo:lo + half]
        acc = _shift_rows(pq, prev, 3) * cw[0:1, lo:lo + half]
        acc = acc + _shift_rows(pq, prev, 2) * cw[1:2, lo:lo + half]
        acc = acc + _shift_rows(pq, prev, 1) * cw[2:3, lo:lo + half]
        acc = acc + pq * cw[3:4, lo:lo + half]
        cbuf_ref[:, lo:lo + half] = pq[tm - halo:tm, :]
        act = jax.nn.silu(acc)
        if c == 0:
            act = act * (MLSTM_DQK ** -0.5)
        qk_ref[:, lo:lo + half] = act.astype(BF16)
        vlo = c * (V_W // 2)
        v_ref[:, vlo:vlo + V_W // 2] = proj(OFF_V + vlo, V_W // 2).astype(BF16)

    for c in range(2):
        lo = c * (V_W // 2)
        o_ref[:, lo:lo + V_W // 2] = jax.nn.sigmoid(proj(OFF_O + lo, V_W // 2))

    cos = cos_ref[...]
    sina = sina_ref[...]
    sinb = sinb_ref[...]
    for c in range(QA_W // 256):
        pa = proj(OFF_QA + c * 256, 256)
        for k in range(2):
            lo = c * 256 + k * LANES
            r = _rope(pa[:, k * LANES:(k + 1) * LANES], cos, sina, sinb)
            qa_ref[:, lo:lo + LANES] = (r * (ATTN_HEAD_DIM ** -0.5)).astype(BF16)

    pkv = proj(OFF_KA, 2 * KV_W)
    ka_ref[...] = _rope(pkv[:, 0:KV_W], cos, sina, sinb)
    va_ref[...] = pkv[:, KV_W:2 * KV_W]


def _inproj(l, x, mod, g, w_packed, conv_qk, bg, cos, sina, sinb):
    b, s, d = x.shape
    tm = _pick_tile(s, 512)
    row = lambda i, j: (i, j, 0)
    lay = lambda i, j: (l, 0, 0)
    outs = [
        (QK_W, BF16), (V_W, BF16), (V_W, F32), (LANES, F32), (QA_W, BF16), (KV_W, F32), (KV_W, F32),
    ]
    return pl.pallas_call(
        _inproj_kernel,
        grid=(b, s // tm),
        in_specs=[
            pl.BlockSpec((None, tm, d), row),
            pl.BlockSpec((None, None, None, 1, d), lambda i, j: (l, i, 0, 0, 0)),
            pl.BlockSpec((None, None, None, 1, d), lambda i, j: (l, i, 1, 0, 0)),
            pl.BlockSpec((None, 1, d), lay),
            pl.BlockSpec((None, d, IN_PACKED), lay),
            pl.BlockSpec((None, MLSTM_CONV, QK_W), lay),
            pl.BlockSpec((None, 1, LANES), lay),
            pl.BlockSpec((None, tm, LANES), row),
            pl.BlockSpec((None, tm, LANES), row),
            pl.BlockSpec((None, tm, LANES), row),
        ],
        out_specs=[pl.BlockSpec((None, tm, w), row) for w, _ in outs],
        out_shape=[jax.ShapeDtypeStruct((b, s, w), dt) for w, dt in outs],
        scratch_shapes=[pltpu.VMEM((tm, d), BF16), pltpu.VMEM((SUBLANES, QK_W), F32)],
        compiler_params=_params("arbitrary", "arbitrary"),
        name="in_proj",
    )(x, mod, mod, g, w_packed, conv_qk, bg, cos, sina, sinb)


def _gate_selector():
    nh = MLSTM_HEADS
    src = jnp.arange(LANES)[:, None]
    group = jnp.arange(3 * nh * LANES)[None, :] // LANES
    base = jnp.asarray([COL_CG, COL_G, COL_B])[group % 3]
    return (src == base + group // 3).astype(BF16)


def _mlstm_kernel(qk_ref, v_ref, o_ref, col_ref, hg_ref, sel_ref, y_ref, c_ref, m_ref):
    L = MLSTM_CHUNK
    nh = MLSTM_HEADS
    n_chunks = qk_ref.shape[0] // L

    @pl.when(pl.program_id(1) == 0)
    def _():
        c_ref[...] = jnp.zeros(c_ref.shape, F32)
        m_ref[...] = jnp.zeros(m_ref.shape, F32)

    ri = lax.broadcasted_iota(jnp.int32, (L, L), 0)
    ci = lax.broadcasted_iota(jnp.int32, (L, L), 1)
    causal = ci <= ri
    ones_col = (lax.broadcasted_iota(jnp.int32, (L, LANES), 1) == 0).astype(BF16)
    nt = (((1,), (1,)), ((), ()))
    tn = (((0,), (0,)), ((), ()))
    units = [(c, h) for c in range(n_chunks) for h in range(nh)]
    rows = [slice(c * L, (c + 1) * L) for c in range(n_chunks)]
    dvs = [slice(h * MLSTM_DV, (h + 1) * MLSTM_DV) for h in range(nh)]

    col = [col_ref[r, :] for r in rows]
    col_t = [cl.T for cl in col]
    q = {(c, h): qk_ref[rows[c], h * MLSTM_DQK:(h + 1) * MLSTM_DQK] for c, h in units}
    k = {(c, h): qk_ref[rows[c], (nh + h) * MLSTM_DQK:(nh + h + 1) * MLSTM_DQK] for c, h in units}
    v_ext = {(c, h): jnp.concatenate([v_ref[rows[c], dvs[h]], ones_col], axis=1) for c, h in units}

    def replicate(x):
        hi = x.astype(BF16)
        rem = x - hi.astype(F32)
        mid = rem.astype(BF16)
        lo = (rem - mid.astype(F32)).astype(BF16)
        sel = sel_ref[...]
        return ((jnp.dot(hi, sel, preferred_element_type=F32)
                 + jnp.dot(mid, sel, preferred_element_type=F32))
                + jnp.dot(lo, sel, preferred_element_type=F32))

    def twice(x):
        return jnp.concatenate([x, x], axis=1)

    rep = [replicate(cl) for cl in col]

    def gate(c, h, kind):
        lo = (3 * h + kind) * LANES
        return rep[c][:, lo:lo + LANES]

    mm, w_inter, decay, w_state = {}, {}, {}, {}
    for h in range(nh):
        m = m_ref[h:h + 1, :]
        for c in range(n_chunks):
            mm[c, h] = jnp.maximum(m, gate(c, h, 0))
            mm_last = mm[c, h][L - 1:L, :]
            w_inter[c, h] = jnp.exp(m - mm[c, h])
            decay[c, h] = jnp.exp(m - mm_last)
            w_state[c, h] = jnp.exp(gate(c, h, 1) - mm_last)
            m = gate(c, h, 2)[L - 1:L, :] + mm_last
        m_ref[h:h + 1, :] = m

    s = {}
    for u in units:
        c, h = u
        g_row = col_t[c][COL_G + h:COL_G + h + 1, :]
        w_intra = jnp.exp(jnp.where(causal, g_row - twice(mm[u]), -jnp.inf))
        s[u] = lax.dot_general(q[u], k[u], nt, preferred_element_type=F32) * w_intra
    intra = {u: jnp.dot(s[u].astype(BF16), v_ext[u][:, 0:MLSTM_DV], preferred_element_type=F32)
             for u in units}
    upd = {u: lax.dot_general((k[u].astype(F32) * w_state[u]).astype(BF16), v_ext[u], tn,
                              preferred_element_type=F32) for u in units}

    c_in = {}
    for h in range(nh):
        state = c_ref[h]
        for c in range(n_chunks):
            c_in[c, h] = state.astype(BF16)
            d = decay[c, h]
            state = jnp.concatenate([d, d, d], axis=1) * state + upd[c, h]
        c_ref[h] = state

    def lane_sum(x):
        return jnp.sum(x[:, 0:LANES] + x[:, LANES:2 * LANES], axis=-1, keepdims=True)

    for u in units:
        c, h = u
        inter = jnp.dot(q[u], c_in[u], preferred_element_type=F32)
        num = intra[u] + twice(w_inter[u]) * inter[:, 0:MLSTM_DV]
        nq = lane_sum(s[u]) + w_inter[u][:, 0:1] * inter[:, MLSTM_DV:MLSTM_DV + 1]
        m_t = gate(c, h, 2)[:, 0:1] + mm[u][:, 0:1]
        rden = 1.0 / jnp.maximum(jnp.abs(nq), jnp.exp(-m_t))
        msq = lane_sum(num * num) * (1.0 / MLSTM_DV)
        scale = rden * lax.rsqrt(rden * rden * msq + NORM_EPS)
        hn = num * scale * hg_ref[:, dvs[h]]
        y_ref[rows[c], dvs[h]] = (o_ref[rows[c], dvs[h]] * hn).astype(BF16)


def _mlstm(l, qk, v, o_sig, cols, head_g, sel):
    b, s, _ = qk.shape
    t = MLSTM_CHUNK
    assert s % t == 0, (s, t)
    row = lambda i, j: (i, j, 0)
    return pl.pallas_call(
        _mlstm_kernel,
        grid=(b, s // t),
        in_specs=[
            pl.BlockSpec((None, t, QK_W), row),
            pl.BlockSpec((None, t, V_W), row),
            pl.BlockSpec((None, t, V_W), row),
            pl.BlockSpec((None, t, LANES), row),
            pl.BlockSpec((None, 1, V_W), lambda i, j: (l, 0, 0)),
            pl.BlockSpec(sel.shape, lambda i, j: (0, 0)),
        ],
        out_specs=pl.BlockSpec((None, t, V_W), row),
        out_shape=jax.ShapeDtypeStruct((b, s, V_W), BF16),
        scratch_shapes=[pltpu.VMEM((MLSTM_HEADS, MLSTM_DQK, MLSTM_DV + LANES), F32),
                        pltpu.VMEM((SUBLANES, LANES), F32)],
        compiler_params=_params("arbitrary", "arbitrary"),
        name="mlstm",
    )(qk, v, o_sig, cols, head_g, sel)


def _block_diag(band, kv_head):
    lane = lax.broadcasted_iota(jnp.int32, band.shape, 1)
    low = lane < ATTN_HEAD_DIM
    rolled = pltpu.roll(band, ATTN_HEAD_DIM, 1)
    if kv_head == 0:
        top = jnp.where(low, band, 0.0)
        bot = jnp.where(low, 0.0, rolled)
    else:
        top = jnp.where(low, rolled, 0.0)
        bot = jnp.where(low, 0.0, band)
    return jnp.concatenate([top, bot], axis=0).astype(BF16)


def _attn_kernel(q_ref, kc_ref, kp_ref, vc_ref, vp_ref, sink_ref, y_ref):
    W = WINDOW
    n_blocks = q_ref.shape[0] // W
    prev_penalty = jnp.where(pl.program_id(1) == 0, -jnp.inf, 0.0)

    key = lax.broadcasted_iota(jnp.int32, (W, PAIRS * W), 0)
    qry = lax.broadcasted_iota(jnp.int32, (W, PAIRS * W), 1) % W
    use_prev = key > qry

    def softmax_t(prev, cur, sink):
        comb = jnp.where(use_prev, prev, cur)
        m = jnp.maximum(jnp.max(comb, axis=0, keepdims=True), sink)
        p = jnp.exp(comb - m)
        den = jnp.sum(p, axis=0, keepdims=True) + jnp.exp(sink - m)
        p = p * (1.0 / den)
        return jnp.where(use_prev, p, 0.0), jnp.where(use_prev, 0.0, p)

    for blk in range(n_blocks):
        rows = slice(blk * W, (blk + 1) * W)
        if blk == 0:
            k_prev, v_prev = kp_ref[...], vp_ref[...]
        else:
            prev = slice((blk - 1) * W, blk * W)
            k_prev, v_prev = kc_ref[prev, :], vc_ref[prev, :]
        k_band = jnp.concatenate([k_prev, kc_ref[rows, :]], axis=0)
        v_band = jnp.concatenate([v_prev, vc_ref[rows, :]], axis=0)
        for kvh in range(ATTN_KV_HEADS):
            k_bd = _block_diag(k_band, kvh)
            v_bd = _block_diag(v_band, kvh)
            lanes = [slice((kvh * PAIRS + p) * LANES, (kvh * PAIRS + p + 1) * LANES)
                     for p in range(PAIRS)]
            q_all = jnp.concatenate([q_ref[rows, ln] for ln in lanes], axis=0)
            st = lax.dot_general(k_bd, q_all, (((1,), (1,)), ((), ())),
                                 preferred_element_type=F32)
            prev_e, cur_e, prev_o, cur_o = (st[i * W:(i + 1) * W, :] for i in range(4))
            if blk == 0:
                prev_e = prev_e + prev_penalty
                prev_o = prev_o + prev_penalty
            pe_prev, pe_cur = softmax_t(prev_e, cur_e, sink_ref[2 * kvh:2 * kvh + 1, :])
            po_prev, po_cur = softmax_t(prev_o, cur_o, sink_ref[2 * kvh + 1:2 * kvh + 2, :])
            pt = jnp.concatenate([pe_prev, pe_cur, po_prev, po_cur], axis=0).astype(BF16)
            out_t = lax.dot_general(v_bd, pt, (((0,), (0,)), ((), ())),
                                    preferred_element_type=F32)
            for p in range(PAIRS):
                y_ref[rows, lanes[p]] = out_t[:, p * W:(p + 1) * W].T.astype(BF16)


def _attention(l, qa, ka, va, sink_rows):
    b, s, _ = qa.shape
    tq = _pick_tile(s, 512)
    per = tq // WINDOW
    cur = lambda i, j: (i, j, 0)
    prev = lambda i, j: (i, jnp.maximum(j * per - 1, 0), 0)
    return pl.pallas_call(
        _attn_kernel,
        grid=(b, s // tq),
        in_specs=[
            pl.BlockSpec((None, tq, QA_W), cur),
            pl.BlockSpec((None, tq, KV_W), cur),
            pl.BlockSpec((None, WINDOW, KV_W), prev),
            pl.BlockSpec((None, tq, KV_W), cur),
            pl.BlockSpec((None, WINDOW, KV_W), prev),
            pl.BlockSpec((None, 2 * ATTN_KV_HEADS, PAIRS * WINDOW), lambda i, j: (l, 0, 0)),
        ],
        out_specs=pl.BlockSpec((None, tq, QA_W), cur),
        out_shape=jax.ShapeDtypeStruct((b, s, QA_W), BF16),
        compiler_params=_params("arbitrary", "arbitrary"),
        name="swa_attention",
    )(qa, ka, ka, va, va, sink_rows)


def _outproj_kernel(ym_ref, ya_ref, x_ref, gate_ref, w_ref, o_ref):
    km = ym_ref.shape[1]
    y = jnp.dot(ym_ref[...], w_ref[0:km, :], preferred_element_type=F32)
    y = y + jnp.dot(ya_ref[...], w_ref[km:, :], preferred_element_type=F32)
    o_ref[...] = x_ref[...] + gate_ref[...] * y


def _outproj(l, ym, ya, x, mod, w_out):
    b, s, d = x.shape
    tm = _pick_tile(s, 512)
    row = lambda i, j: (i, j, 0)
    return pl.pallas_call(
        _outproj_kernel,
        grid=(b, s // tm),
        in_specs=[
            pl.BlockSpec((None, tm, ym.shape[2]), row),
            pl.BlockSpec((None, tm, ya.shape[2]), row),
            pl.BlockSpec((None, tm, d), row),
            pl.BlockSpec((None, None, None, 1, d), lambda i, j: (l, i, 2, 0, 0)),
            pl.BlockSpec((None,) + w_out.shape[1:], lambda i, j: (l, 0, 0)),
        ],
        out_specs=pl.BlockSpec((None, tm, d), row),
        out_shape=jax.ShapeDtypeStruct((b, s, d), F32),
        compiler_params=_params("arbitrary", "arbitrary"),
        name="out_proj",
    )(ym, ya, x, mod, w_out)


def _shift_rows(x, prev_tail, k):
    rolled = pltpu.roll(x, k, 0)
    sub = lax.broadcasted_iota(jnp.int32, prev_tail.shape, 0)
    head = jnp.where(sub < k, pltpu.roll(prev_tail, k, 0), rolled[0:SUBLANES, :])
    return jnp.concatenate([head, rolled[SUBLANES:, :]], axis=0)


def _ffn_kernel(x_ref, shift_ref, scale_ref, gate_ref, g_ref, fg_ref, wg_ref, wu_ref, cw_ref, wd_ref,
                o_ref, hb_ref, gs_ref, us_ref, act_ref, halo_ref, tail_ref, *, nf, final_norm):
    tm = x_ref.shape[0]
    halo = SUBLANES
    i = pl.program_id(1)
    j = pl.program_id(2)

    def up_g():
        gs_ref[...] = jnp.dot(hb_ref[...], wg_ref[...], preferred_element_type=F32)

    def up_u():
        us_ref[...] = jnp.dot(hb_ref[...], wu_ref[...], preferred_element_type=F32)

    def fill_halo():
        @pl.when(i == 0)
        def _():
            halo_ref[...] = jnp.zeros(halo_ref.shape, F32)

        @pl.when(i > 0)
        def _():
            halo_ref[...] = tail_ref[j - 1]

    def gated():
        cw = cw_ref[...]
        g = gs_ref[...]
        prev = halo_ref[...]
        conv = _shift_rows(g, prev, 2) * cw[0:1, :]
        conv = conv + _shift_rows(g, prev, 1) * cw[1:2, :]
        conv = conv + g * cw[2:3, :]
        tail_ref[j - 1] = g[tm - halo:tm, :]
        act_ref[...] = (jax.nn.silu(conv) * us_ref[...]).astype(BF16)

    def down():
        o_ref[...] += jnp.dot(act_ref[...], wd_ref[...], preferred_element_type=F32)

    @pl.when(j == 0)
    def _():
        hb_ref[...] = _adaln(x_ref[...], g_ref[...], shift_ref[...], scale_ref[...]).astype(BF16)
        o_ref[...] = jnp.zeros(o_ref.shape, F32)
        up_g()
        up_u()

    @pl.when((j > 0) & (j < nf))
    def _():
        fill_halo()
        gated()
        up_g()
        down()
        up_u()

    @pl.when(j == nf)
    def _():
        fill_halo()
        gated()
        down()
        res = x_ref[...] + gate_ref[...] * o_ref[...]
        if final_norm:
            ms = jnp.mean(res * res, axis=-1, keepdims=True)
            res = res * lax.rsqrt(ms + NORM_EPS) * fg_ref[...]
        o_ref[...] = res


def _ffn(l, x, mod, g, final_g, w_up, conv_ffn, w_down, final_norm):
    b, s, d = x.shape
    ff = w_down.shape[1]
    tm, tf = _ffn_tiles(s, ff)
    nf = ff // tf
    row = lambda i, t, j: (i, t, 0)
    mod_spec = lambda k: pl.BlockSpec((None, None, None, 1, d), lambda i, t, j: (l, i, k, 0, 0))
    up_tile = lambda j: jnp.minimum(j, nf - 1)
    down_tile = lambda j: jnp.maximum(j - 1, 0)
    return pl.pallas_call(
        functools.partial(_ffn_kernel, nf=nf, final_norm=final_norm),
        grid=(b, s // tm, nf + 1),
        in_specs=[
            pl.BlockSpec((None, tm, d), row),
            mod_spec(3), mod_spec(4), mod_spec(5),
            pl.BlockSpec((None, 1, d), lambda i, t, j: (l, 0, 0)),
            pl.BlockSpec((1, d), lambda i, t, j: (0, 0)),
            pl.BlockSpec((None, d, tf), lambda i, t, j: (l, 0, up_tile(j))),
            pl.BlockSpec((None, d, tf), lambda i, t, j: (l, 0, nf + up_tile(j))),
            pl.BlockSpec((None, FFN_CONV, tf), lambda i, t, j: (l, 0, down_tile(j))),
            pl.BlockSpec((None, tf, d), lambda i, t, j: (l, down_tile(j), 0)),
        ],
        out_specs=pl.BlockSpec((None, tm, d), row),
        out_shape=jax.ShapeDtypeStruct((b, s, d), F32),
        scratch_shapes=[pltpu.VMEM((tm, d), BF16),
                        pltpu.VMEM((tm, tf), F32),
                        pltpu.VMEM((tm, tf), F32),
                        pltpu.VMEM((tm, tf), BF16),
                        pltpu.VMEM((SUBLANES, tf), F32),
                        pltpu.VMEM((nf, SUBLANES, tf), F32)],
        compiler_params=_params("arbitrary", "arbitrary", "arbitrary"),
        name="convglu_ffn",
    )(x, mod, mod, mod, g, final_g, w_up, w_up, conv_ffn, w_down)


def _pack_w_in(w_in):
    nh = MLSTM_HEADS
    o_gates = QK_W + 2 * V_W
    o_attn = o_gates + 2 * nh
    gates = w_in[:, :, o_gates:o_attn]
    gates = jnp.pad(gates, ((0, 0), (0, 0), (0, LANES - 2 * nh)))
    packed = jnp.concatenate([w_in[:, :, :o_gates], w_in[:, :, o_attn:], gates], axis=-1)
    return packed.astype(BF16)


def _sink_rows(sinks):
    depth = sinks.shape[0]
    t = sinks.reshape(depth, ATTN_KV_HEADS, PAIRS, 2).transpose(0, 1, 3, 2)
    t = jnp.repeat(t[..., None], WINDOW, axis=-1)
    return t.reshape(depth, 2 * ATTN_KV_HEADS, PAIRS * WINDOW)


def kernel(x, c, positions, ada_w, ada_b, norm_mix_g, w_in, b_gates, conv_qk, mlstm_head_g, sinks,
           w_out, norm_ffn_g, w_up, conv_ffn, w_down, final_g):
    b, s, d = x.shape
    depth = ada_w.shape[0]
    nh = MLSTM_HEADS

    c_pad = jnp.pad(c, ((0, 2 * SUBLANES - b), (0, 0)))
    mod = _modulation(c_pad, ada_w, ada_b)[:, :b].reshape(depth, b, 6, 1, d)
    cos, sina, sinb = _rope_tables(positions)

    w_in_p = _pack_w_in(w_in)
    w_out_b = w_out.astype(BF16)
    w_up_b = w_up.astype(BF16)
    w_down_b = w_down.astype(BF16)
    bg = jnp.pad(b_gates, ((0, 0), (0, LANES - 2 * nh))).reshape(depth, 1, LANES)
    norm_mix = norm_mix_g.reshape(depth, 1, d)
    norm_ffn = norm_ffn_g.reshape(depth, 1, d)
    head_g = mlstm_head_g.reshape(depth, 1, V_W)
    sink_rows = _sink_rows(sinks)
    gate_sel = _gate_selector()
    final_g2 = final_g.reshape(1, d)

    for l in range(depth):
        qk, v, o_sig, cols, qa, ka, va = _inproj(
            l, x, mod, norm_mix, w_in_p, conv_qk, bg, cos, sina, sinb)
        ym = _mlstm(l, qk, v, o_sig, cols, head_g, gate_sel)
        ya = _attention(l, qa, ka, va, sink_rows)
        x = _outproj(l, ym, ya, x, mod, w_out_b)
        x = _ffn(l, x, mod, norm_ffn, final_g2, w_up_b, conv_ffn, w_down_b,
                 final_norm=(l == depth - 1))
    return x
```

```python
import functools

import jax
import jax.numpy as jnp
from jax import lax
from jax.experimental import pallas as pl
from jax.experimental.pallas import tpu as pltpu

F32 = jnp.float32
BF16 = jnp.bfloat16

MLSTM_HEADS = 4
MLSTM_DQK = 128
MLSTM_DV = 256
MLSTM_CONV = 4
MLSTM_CHUNK = 256
GATE_SOFTCAP = 15.0
ATTN_HEAD_DIM = 64
ATTN_HEADS = 16
ATTN_KV_HEADS = 2
WINDOW = 128
ROPE_DIM = 16
ROPE_THETA = 500000.0
FFN_CONV = 3
FFN_ROW_BLOCKS = 4
NORM_EPS = 1e-6

LANES = 128
SUBLANES = 8
VMEM_LIMIT = 61 * 1024 * 1024

QK_W = 2 * MLSTM_HEADS * MLSTM_DQK
V_W = MLSTM_HEADS * MLSTM_DV
QA_W = ATTN_HEADS * ATTN_HEAD_DIM
KV_W = ATTN_KV_HEADS * ATTN_HEAD_DIM
OFF_QK = 0
OFF_V = OFF_QK + QK_W
OFF_O = OFF_V + V_W
OFF_QA = OFF_O + V_W
OFF_KA = OFF_QA + QA_W
OFF_VA = OFF_KA + KV_W
OFF_G = OFF_VA + KV_W
IN_PACKED = OFF_G + LANES

COL_G = 0
COL_B = MLSTM_HEADS
COL_CG = 2 * MLSTM_HEADS
PAIRS = ATTN_HEADS // ATTN_KV_HEADS // 2


def _params(*sem):
    return pltpu.CompilerParams(dimension_semantics=sem, vmem_limit_bytes=VMEM_LIMIT)


def _pick_tile(n, pref):
    t = min(n, pref)
    while n % t:
        t //= 2
    return t


def _ffn_tiles(s, ff):
    return _pick_tile(s, 1024), _pick_tile(ff, 512)


def _mod_kernel(c_ref, w_ref, b_ref, o_ref):
    ca = jax.nn.silu(c_ref[...]).astype(BF16)
    o_ref[...] = jnp.dot(ca, w_ref[...].astype(BF16), preferred_element_type=F32) + b_ref[...]


def _modulation(c_pad, ada_w, ada_b):
    depth, d, n = ada_w.shape
    rows = c_pad.shape[0]
    tn = _pick_tile(n, 1024)
    return pl.pallas_call(
        _mod_kernel,
        grid=(depth, n // tn),
        in_specs=[
            pl.BlockSpec((rows, d), lambda l, j: (0, 0)),
            pl.BlockSpec((None, d, tn), lambda l, j: (l, 0, j)),
            pl.BlockSpec((None, 1, tn), lambda l, j: (l, 0, j)),
        ],
        out_specs=pl.BlockSpec((None, rows, tn), lambda l, j: (l, 0, j)),
        out_shape=jax.ShapeDtypeStruct((depth, rows, n), F32),
        compiler_params=_params("arbitrary", "arbitrary"),
        name="adaln_mod",
    )(c_pad, ada_w, ada_b.reshape(depth, 1, n))


def _rope_kernel(pos_ref, freq_ref, cos_ref, sina_ref, sinb_ref):
    ang = pos_ref[...].astype(F32) * freq_ref[...]
    d = lax.broadcasted_iota(jnp.int32, ang.shape, 1) % ATTN_HEAD_DIM
    c = jnp.cos(ang)
    s = jnp.sin(ang)
    half = ROPE_DIM // 2
    cos_ref[...] = jnp.where(d < ROPE_DIM, c, 1.0)
    sina_ref[...] = jnp.where(d < half, -s, 0.0)
    sinb_ref[...] = jnp.where((d >= half) & (d < ROPE_DIM), s, 0.0)


def _rope_tables(positions):
    b, s = positions.shape
    ts = _pick_tile(s, 1024)
    inv_freq = ROPE_THETA ** (-jnp.arange(0, ROPE_DIM, 2, dtype=F32) / ROPE_DIM)
    per_head = jnp.concatenate(
        [inv_freq, inv_freq, jnp.zeros((ATTN_HEAD_DIM - ROPE_DIM,), F32)])
    freq = jnp.tile(per_head, LANES // ATTN_HEAD_DIM).reshape(1, LANES)
    tab = jax.ShapeDtypeStruct((b, s, LANES), F32)
    spec = pl.BlockSpec((None, ts, LANES), lambda i, j: (i, j, 0))
    return pl.pallas_call(
        _rope_kernel,
        grid=(b, s // ts),
        in_specs=[pl.BlockSpec((None, ts, 1), lambda i, j: (i, j, 0)),
                  pl.BlockSpec((1, LANES), lambda i, j: (0, 0))],
        out_specs=[spec, spec, spec],
        out_shape=[tab, tab, tab],
        compiler_params=_params("arbitrary", "arbitrary"),
        name="rope_tables",
    )(positions.reshape(b, s, 1), freq)


def _adaln(x, g, shift, scale):
    ms = jnp.mean(x * x, axis=-1, keepdims=True)
    h = x * lax.rsqrt(ms + NORM_EPS) * g
    return h * (1.0 + scale) + shift


def _rope(t, cos, sina, sinb):
    n = t.shape[-1]
    half = ROPE_DIM // 2
    return t * cos + pltpu.roll(t, n - half, 1) * sina + pltpu.roll(t, half, 1) * sinb


def _chunk_scan_rows(x, op, fill):
    pos = lax.broadcasted_iota(jnp.int32, x.shape, 0) % MLSTM_CHUNK
    step = 1
    while step < MLSTM_CHUNK:
        x = op(x, jnp.where(pos >= step, pltpu.roll(x, step, 0), fill))
        step *= 2
    return x


def _gate_columns(gates):
    nh = MLSTM_HEADS
    lane = lax.broadcasted_iota(jnp.int32, gates.shape, 1)
    b = _chunk_scan_rows(jax.nn.log_sigmoid(gates), jnp.add, 0.0)
    g = gates - pltpu.roll(b, LANES - nh, 1)
    cg = _chunk_scan_rows(g, jnp.maximum, -jnp.inf)
    return jnp.where(lane < COL_B, g, jnp.where(lane < COL_CG, b, pltpu.roll(cg, COL_CG, 1)))


def _inproj_kernel(x_ref, shift_ref, scale_ref, g_ref, w_ref, cw_ref, bg_ref,
                   cos_ref, sina_ref, sinb_ref,
                   qk_ref, v_ref, o_ref, col_ref, qa_ref, ka_ref, va_ref,
                   hb_ref, cbuf_ref):
    tm = x_ref.shape[0]
    halo = SUBLANES

    @pl.when(pl.program_id(1) == 0)
    def _():
        cbuf_ref[...] = jnp.zeros(cbuf_ref.shape, F32)

    hb_ref[...] = _adaln(x_ref[...], g_ref[...], shift_ref[...], scale_ref[...]).astype(BF16)

    def proj(off, width):
        return jnp.dot(hb_ref[...], w_ref[:, off:off + width], preferred_element_type=F32)

    pg = proj(OFF_G, LANES) + bg_ref[...]
    col_ref[...] = _gate_columns(GATE_SOFTCAP * jnp.tanh(pg / GATE_SOFTCAP))

    cw = cw_ref[...]
    half = QK_W // 2
    for c in range(2):
        lo = c * half
        pq = proj(OFF_QK + lo, half)
        prev = cbuf_ref[:, lo:lo + half]
        acc = _shift_rows(pq, prev, 3) * cw[0:1, lo:lo + half]
        acc = acc + _shift_rows(pq, prev, 2) * cw[1:2, lo:lo + half]
        acc = acc + _shift_rows(pq, prev, 1) * cw[2:3, lo:lo + half]
        acc = acc + pq * cw[3:4, lo:lo + half]
        cbuf_ref[:, lo:lo + half] = pq[tm - halo:tm, :]
        act = jax.nn.silu(acc)
        if c == 0:
            act = act * (MLSTM_DQK ** -0.5)
        qk_ref[:, lo:lo + half] = act.astype(BF16)
        vlo = c * (V_W // 2)
        v_ref[:, vlo:vlo + V_W // 2] = proj(OFF_V + vlo, V_W // 2).astype(BF16)

    for c in range(2):
        lo = c * (V_W // 2)
        o_ref[:, lo:lo + V_W // 2] = jax.nn.sigmoid(proj(OFF_O + lo, V_W // 2))

    cos = cos_ref[...]
    sina = sina_ref[...]
    sinb = sinb_ref[...]
    for c in range(QA_W // 256):
        pa = proj(OFF_QA + c * 256, 256)
        for k in range(2):
            lo = c * 256 + k * LANES
            r = _rope(pa[:, k * LANES:(k + 1) * LANES], cos, sina, sinb)
            qa_ref[:, lo:lo + LANES] = (r * (ATTN_HEAD_DIM ** -0.5)).astype(BF16)

    pkv = proj(OFF_KA, 2 * KV_W)
    ka_ref[...] = _rope(pkv[:, 0:KV_W], cos, sina, sinb)
    va_ref[...] = pkv[:, KV_W:2 * KV_W]


def _inproj(l, x, mod, g, w_packed, conv_qk, bg, cos, sina, sinb):
    b, s, d = x.shape
    tm = _pick_tile(s, 512)
    row = lambda i, j: (i, j, 0)
    lay = lambda i, j: (l, 0, 0)
    outs = [
        (QK_W, BF16), (V_W, BF16), (V_W, F32), (LANES, F32), (QA_W, BF16), (KV_W, F32), (KV_W, F32),
    ]
    return pl.pallas_call(
        _inproj_kernel,
        grid=(b, s // tm),
        in_specs=[
            pl.BlockSpec((None, tm, d), row),
            pl.BlockSpec((None, None, None, 1, d), lambda i, j: (l, i, 0, 0, 0)),
            pl.BlockSpec((None, None, None, 1, d), lambda i, j: (l, i, 1, 0, 0)),
            pl.BlockSpec((None, 1, d), lay),
            pl.BlockSpec((None, d, IN_PACKED), lay),
            pl.BlockSpec((None, MLSTM_CONV, QK_W), lay),
            pl.BlockSpec((None, 1, LANES), lay),
            pl.BlockSpec((None, tm, LANES), row),
            pl.BlockSpec((None, tm, LANES), row),
            pl.BlockSpec((None, tm, LANES), row),
        ],
        out_specs=[pl.BlockSpec((None, tm, w), row) for w, _ in outs],
        out_shape=[jax.ShapeDtypeStruct((b, s, w), dt) for w, dt in outs],
        scratch_shapes=[pltpu.VMEM((tm, d), BF16), pltpu.VMEM((SUBLANES, QK_W), F32)],
        compiler_params=_params("arbitrary", "arbitrary"),
        name="in_proj",
    )(x, mod, mod, g, w_packed, conv_qk, bg, cos, sina, sinb)


def _gate_selector():
    nh = MLSTM_HEADS
    src = jnp.arange(LANES)[:, None]
    group = jnp.arange(3 * nh * LANES)[None, :] // LANES
    base = jnp.asarray([COL_CG, COL_G, COL_B])[group % 3]
    return (src == base + group // 3).astype(BF16)


def _mlstm_kernel(qk_ref, v_ref, o_ref, col_ref, hg_ref, sel_ref, y_ref, c_ref, m_ref):
    L = MLSTM_CHUNK
    nh = MLSTM_HEADS
    n_chunks = qk_ref.shape[0] // L

    @pl.when(pl.program_id(1) == 0)
    def _():
        c_ref[...] = jnp.zeros(c_ref.shape, F32)
        m_ref[...] = jnp.zeros(m_ref.shape, F32)

    ri = lax.broadcasted_iota(jnp.int32, (L, L), 0)
    ci = lax.broadcasted_iota(jnp.int32, (L, L), 1)
    causal = ci <= ri
    ones_col = (lax.broadcasted_iota(jnp.int32, (L, LANES), 1) == 0).astype(BF16)
    nt = (((1,), (1,)), ((), ()))
    tn = (((0,), (0,)), ((), ()))
    units = [(c, h) for c in range(n_chunks) for h in range(nh)]
    rows = [slice(c * L, (c + 1) * L) for c in range(n_chunks)]
    dvs = [slice(h * MLSTM_DV, (h + 1) * MLSTM_DV) for h in range(nh)]

    col = [col_ref[r, :] for r in rows]
    col_t = [cl.T for cl in col]
    q = {(c, h): qk_ref[rows[c], h * MLSTM_DQK:(h + 1) * MLSTM_DQK] for c, h in units}
    k = {(c, h): qk_ref[rows[c], (nh + h) * MLSTM_DQK:(nh + h + 1) * MLSTM_DQK] for c, h in units}
    v_ext = {(c, h): jnp.concatenate([v_ref[rows[c], dvs[h]], ones_col], axis=1) for c, h in units}

    def replicate(x):
        hi = x.astype(BF16)
        rem = x - hi.astype(F32)
        mid = rem.astype(BF16)
        lo = (rem - mid.astype(F32)).astype(BF16)
        sel = sel_ref[...]
        return ((jnp.dot(hi, sel, preferred_element_type=F32)
                 + jnp.dot(mid, sel, preferred_element_type=F32))
                + jnp.dot(lo, sel, preferred_element_type=F32))

    def twice(x):
        return jnp.concatenate([x, x], axis=1)

    rep = [replicate(cl) for cl in col]

    def gate(c, h, kind):
        lo = (3 * h + kind) * LANES
        return rep[c][:, lo:lo + LANES]

    mm, w_inter, decay, w_state = {}, {}, {}, {}
    for h in range(nh):
        m = m_ref[h:h + 1, :]
        for c in range(n_chunks):
            mm[c, h] = jnp.maximum(m, gate(c, h, 0))
            mm_last = mm[c, h][L - 1:L, :]
            w_inter[c, h] = jnp.exp(m - mm[c, h])
            decay[c, h] = jnp.exp(m - mm_last)
            w_state[c, h] = jnp.exp(gate(c, h, 1) - mm_last)
            m = gate(c, h, 2)[L - 1:L, :] + mm_last
        m_ref[h:h + 1, :] = m

    s = {}
    for u in units:
        c, h = u
        g_row = col_t[c][COL_G + h:COL_G + h + 1, :]
        w_intra = jnp.exp(jnp.where(causal, g_row - twice(mm[u]), -jnp.inf))
        s[u] = lax.dot_general(q[u], k[u], nt, preferred_element_type=F32) * w_intra
    intra = {u: jnp.dot(s[u].astype(BF16), v_ext[u][:, 0:MLSTM_DV], preferred_element_type=F32)
             for u in units}
    upd = {u: lax.dot_general((k[u].astype(F32) * w_state[u]).astype(BF16), v_ext[u], tn,
                              preferred_element_type=F32) for u in units}

    c_in = {}
    for h in range(nh):
        state = c_ref[h]
        for c in range(n_chunks):
            c_in[c, h] = state.astype(BF16)
            d = decay[c, h]
            state = jnp.concatenate([d, d, d], axis=1) * state + upd[c, h]
        c_ref[h] = state

    def lane_sum(x):
        return jnp.sum(x[:, 0:LANES] + x[:, LANES:2 * LANES], axis=-1, keepdims=True)

    for u in units:
        c, h = u
        inter = jnp.dot(q[u], c_in[u], preferred_element_type=F32)
        num = intra[u] + twice(w_inter[u]) * inter[:, 0:MLSTM_DV]
        nq = lane_sum(s[u]) + w_inter[u][:, 0:1] * inter[:, MLSTM_DV:MLSTM_DV + 1]
        m_t = gate(c, h, 2)[:, 0:1] + mm[u][:, 0:1]
        rden = 1.0 / jnp.maximum(jnp.abs(nq), jnp.exp(-m_t))
        msq = lane_sum(num * num) * (1.0 / MLSTM_DV)
        scale = rden * lax.rsqrt(rden * rden * msq + NORM_EPS)
        hn = num * scale * hg_ref[:, dvs[h]]
        y_ref[rows[c], dvs[h]] = (o_ref[rows[c], dvs[h]] * hn).astype(BF16)


def _mlstm(l, qk, v, o_sig, cols, head_g, sel):
    b, s, _ = qk.shape
    assert s % MLSTM_CHUNK == 0, (s, MLSTM_CHUNK)
    t = _pick_tile(s // MLSTM_CHUNK, 2) * MLSTM_CHUNK
    row = lambda i, j: (i, j, 0)
    return pl.pallas_call(
        _mlstm_kernel,
        grid=(b, s // t),
        in_specs=[
            pl.BlockSpec((None, t, QK_W), row),
            pl.BlockSpec((None, t, V_W), row),
            pl.BlockSpec((None, t, V_W), row),
            pl.BlockSpec((None, t, LANES), row),
            pl.BlockSpec((None, 1, V_W), lambda i, j: (l, 0, 0)),
            pl.BlockSpec(sel.shape, lambda i, j: (0, 0)),
        ],
        out_specs=pl.BlockSpec((None, t, V_W), row),
        out_shape=jax.ShapeDtypeStruct((b, s, V_W), BF16),
        scratch_shapes=[pltpu.VMEM((MLSTM_HEADS, MLSTM_DQK, MLSTM_DV + LANES), F32),
                        pltpu.VMEM((SUBLANES, LANES), F32)],
        compiler_params=_params("arbitrary", "arbitrary"),
        name="mlstm",
    )(qk, v, o_sig, cols, head_g, sel)


def _block_diag(band, kv_head):
    lane = lax.broadcasted_iota(jnp.int32, band.shape, 1)
    low = lane < ATTN_HEAD_DIM
    rolled = pltpu.roll(band, ATTN_HEAD_DIM, 1)
    if kv_head == 0:
        top = jnp.where(low, band, 0.0)
        bot = jnp.where(low, 0.0, rolled)
    else:
        top = jnp.where(low, rolled, 0.0)
        bot = jnp.where(low, 0.0, band)
    return jnp.concatenate([top, bot], axis=0).astype(BF16)


def _attn_kernel(q_ref, kc_ref, kp_ref, vc_ref, vp_ref, sink_ref, y_ref):
    W = WINDOW
    n_blocks = q_ref.shape[0] // W
    prev_penalty = jnp.where(pl.program_id(1) == 0, -jnp.inf, 0.0)

    key = lax.broadcasted_iota(jnp.int32, (W, PAIRS * W), 0)
    qry = lax.broadcasted_iota(jnp.int32, (W, PAIRS * W), 1) % W
    use_prev = key > qry

    def softmax_t(prev, cur, sink):
        comb = jnp.where(use_prev, prev, cur)
        m = jnp.maximum(jnp.max(comb, axis=0, keepdims=True), sink)
        p = jnp.exp(comb - m)
        den = jnp.sum(p, axis=0, keepdims=True) + jnp.exp(sink - m)
        p = p * (1.0 / den)
        return jnp.where(use_prev, p, 0.0), jnp.where(use_prev, 0.0, p)

    for blk in range(n_blocks):
        rows = slice(blk * W, (blk + 1) * W)
        if blk == 0:
            k_prev, v_prev = kp_ref[...], vp_ref[...]
        else:
            prev = slice((blk - 1) * W, blk * W)
            k_prev, v_prev = kc_ref[prev, :], vc_ref[prev, :]
        k_band = jnp.concatenate([k_prev, kc_ref[rows, :]], axis=0)
        v_band = jnp.concatenate([v_prev, vc_ref[rows, :]], axis=0)
        for kvh in range(ATTN_KV_HEADS):
            k_bd = _block_diag(k_band, kvh)
            v_bd = _block_diag(v_band, kvh)
            lanes = [slice((kvh * PAIRS + p) * LANES, (kvh * PAIRS + p + 1) * LANES)
                     for p in range(PAIRS)]
            q_all = jnp.concatenate([q_ref[rows, ln] for ln in lanes], axis=0)
            st = lax.dot_general(k_bd, q_all, (((1,), (1,)), ((), ())),
                                 preferred_element_type=F32)
            prev_e, cur_e, prev_o, cur_o = (st[i * W:(i + 1) * W, :] for i in range(4))
            if blk == 0:
                prev_e = prev_e + prev_penalty
                prev_o = prev_o + prev_penalty
            pe_prev, pe_cur = softmax_t(prev_e, cur_e, sink_ref[2 * kvh:2 * kvh + 1, :])
            po_prev, po_cur = softmax_t(prev_o, cur_o, sink_ref[2 * kvh + 1:2 * kvh + 2, :])
            pt = jnp.concatenate([pe_prev, pe_cur, po_prev, po_cur], axis=0).astype(BF16)
            out_t = lax.dot_general(v_bd, pt, (((0,), (0,)), ((), ())),
                                    preferred_element_type=F32)
            for p in range(PAIRS):
                y_ref[rows, lanes[p]] = out_t[:, p * W:(p + 1) * W].T.astype(BF16)


def _attention(l, qa, ka, va, sink_rows):
    b, s, _ = qa.shape
    tq = _pick_tile(s, 512)
    per = tq // WINDOW
    cur = lambda i, j: (i, j, 0)
    prev = lambda i, j: (i, jnp.maximum(j * per - 1, 0), 0)
    return pl.pallas_call(
        _attn_kernel,
        grid=(b, s // tq),
        in_specs=[
            pl.BlockSpec((None, tq, QA_W), cur),
            pl.BlockSpec((None, tq, KV_W), cur),
            pl.BlockSpec((None, WINDOW, KV_W), prev),
            pl.BlockSpec((None, tq, KV_W), cur),
            pl.BlockSpec((None, WINDOW, KV_W), prev),
            pl.BlockSpec((None, 2 * ATTN_KV_HEADS, PAIRS * WINDOW), lambda i, j: (l, 0, 0)),
        ],
        out_specs=pl.BlockSpec((None, tq, QA_W), cur),
        out_shape=jax.ShapeDtypeStruct((b, s, QA_W), BF16),
        compiler_params=_params("arbitrary", "arbitrary"),
        name="swa_attention",
    )(qa, ka, ka, va, va, sink_rows)


def _outproj_kernel(ym_ref, ya_ref, x_ref, gate_ref, w_ref, o_ref):
    km = ym_ref.shape[1]
    y = jnp.dot(ym_ref[...], w_ref[0:km, :], preferred_element_type=F32)
    y = y + jnp.dot(ya_ref[...], w_ref[km:, :], preferred_element_type=F32)
    o_ref[...] = x_ref[...] + gate_ref[...] * y


def _outproj(l, ym, ya, x, mod, w_out):
    b, s, d = x.shape
    tm = _pick_tile(s, 512)
    row = lambda i, j: (i, j, 0)
    return pl.pallas_call(
        _outproj_kernel,
        grid=(b, s // tm),
        in_specs=[
            pl.BlockSpec((None, tm, ym.shape[2]), row),
            pl.BlockSpec((None, tm, ya.shape[2]), row),
            pl.BlockSpec((None, tm, d), row),
            pl.BlockSpec((None, None, None, 1, d), lambda i, j: (l, i, 2, 0, 0)),
            pl.BlockSpec((None,) + w_out.shape[1:], lambda i, j: (l, 0, 0)),
        ],
        out_specs=pl.BlockSpec((None, tm, d), row),
        out_shape=jax.ShapeDtypeStruct((b, s, d), F32),
        compiler_params=_params("arbitrary", "arbitrary"),
        name="out_proj",
    )(ym, ya, x, mod, w_out)


def _shift_rows(x, prev_tail, k):
    rolled = pltpu.roll(x, k, 0)
    sub = lax.broadcasted_iota(jnp.int32, prev_tail.shape, 0)
    head = jnp.where(sub < k, pltpu.roll(prev_tail, k, 0), rolled[0:SUBLANES, :])
    return jnp.concatenate([head, rolled[SUBLANES:, :]], axis=0)


def _ffn_kernel(x_ref, shift_ref, scale_ref, gate_ref, g_ref, fg_ref, wg_ref, wu_ref, cw_ref, wd_ref,
                o_ref, hb_ref, gs_ref, us_ref, halo_ref, tail_ref, *, nf, final_norm):
    tm = x_ref.shape[0]
    halo = SUBLANES
    i = pl.program_id(1)
    j = pl.program_id(2)

    def up_g():
        gs_ref[...] = jnp.dot(hb_ref[...], wg_ref[...], preferred_element_type=F32)

    def up_u():
        us_ref[...] = jnp.dot(hb_ref[...], wu_ref[...], preferred_element_type=F32)

    def fill_halo():
        @pl.when(i == 0)
        def _():
            halo_ref[...] = jnp.zeros(halo_ref.shape, F32)

        @pl.when(i > 0)
        def _():
            halo_ref[...] = tail_ref[j - 1]

    n_rb = FFN_ROW_BLOCKS if tm % (FFN_ROW_BLOCKS * SUBLANES) == 0 else 1
    rb = tm // n_rb

    def gated_down():
        cw = cw_ref[...]
        for r in range(n_rb):
            rows = slice(r * rb, (r + 1) * rb)
            g = gs_ref[rows, :]
            prev = halo_ref[...] if r == 0 else gs_ref[r * rb - halo:r * rb, :]
            conv = _shift_rows(g, prev, 2) * cw[0:1, :]
            conv = conv + _shift_rows(g, prev, 1) * cw[1:2, :]
            conv = conv + g * cw[2:3, :]
            act = (jax.nn.silu(conv) * us_ref[rows, :]).astype(BF16)
            o_ref[rows, :] += jnp.dot(act, wd_ref[...], preferred_element_type=F32)
        tail_ref[j - 1] = gs_ref[tm - halo:tm, :]

    @pl.when(j == 0)
    def _():
        hb_ref[...] = _adaln(x_ref[...], g_ref[...], shift_ref[...], scale_ref[...]).astype(BF16)
        o_ref[...] = jnp.zeros(o_ref.shape, F32)
        up_g()
        up_u()

    @pl.when((j > 0) & (j < nf))
    def _():
        fill_halo()
        gated_down()
        up_g()
        up_u()

    @pl.when(j == nf)
    def _():
        fill_halo()
        gated_down()
        res = x_ref[...] + gate_ref[...] * o_ref[...]
        if final_norm:
            ms = jnp.mean(res * res, axis=-1, keepdims=True)
            res = res * lax.rsqrt(ms + NORM_EPS) * fg_ref[...]
        o_ref[...] = res


def _ffn(l, x, mod, g, final_g, w_up, conv_ffn, w_down, final_norm):
    b, s, d = x.shape
    ff = w_down.shape[1]
    tm, tf = _ffn_tiles(s, ff)
    nf = ff // tf
    row = lambda i, t, j: (i, t, 0)
    mod_spec = lambda k: pl.BlockSpec((None, None, None, 1, d), lambda i, t, j: (l, i, k, 0, 0))
    up_tile = lambda j: jnp.minimum(j, nf - 1)
    down_tile = lambda j: jnp.maximum(j - 1, 0)
    return pl.pallas_call(
        functools.partial(_ffn_kernel, nf=nf, final_norm=final_norm),
        grid=(b, s // tm, nf + 1),
        in_specs=[
            pl.BlockSpec((None, tm, d), row),
            mod_spec(3), mod_spec(4), mod_spec(5),
            pl.BlockSpec((None, 1, d), lambda i, t, j: (l, 0, 0)),
            pl.BlockSpec((1, d), lambda i, t, j: (0, 0)),
            pl.BlockSpec((None, d, tf), lambda i, t, j: (l, 0, up_tile(j))),
            pl.BlockSpec((None, d, tf), lambda i, t, j: (l, 0, nf + up_tile(j))),
            pl.BlockSpec((None, FFN_CONV, tf), lambda i, t, j: (l, 0, down_tile(j))),
            pl.BlockSpec((None, tf, d), lambda i, t, j: (l, down_tile(j), 0)),
        ],
        out_specs=pl.BlockSpec((None, tm, d), row),
        out_shape=jax.ShapeDtypeStruct((b, s, d), F32),
        scratch_shapes=[pltpu.VMEM((tm, d), BF16),
                        pltpu.VMEM((tm, tf), F32),
                        pltpu.VMEM((tm, tf), F32),
                        pltpu.VMEM((SUBLANES, tf), F32),
                        pltpu.VMEM((nf, SUBLANES, tf), F32)],
        compiler_params=_params("arbitrary", "arbitrary", "arbitrary"),
        name="convglu_ffn",
    )(x, mod, mod, mod, g, final_g, w_up, w_up, conv_ffn, w_down)


def _pack_w_in(w_in):
    nh = MLSTM_HEADS
    o_gates = QK_W + 2 * V_W
    o_attn = o_gates + 2 * nh
    gates = w_in[:, :, o_gates:o_attn]
    gates = jnp.pad(gates, ((0, 0), (0, 0), (0, LANES - 2 * nh)))
    packed = jnp.concatenate([w_in[:, :, :o_gates], w_in[:, :, o_attn:], gates], axis=-1)
    return packed.astype(BF16)


def _sink_rows(sinks):
    depth = sinks.shape[0]
    t = sinks.reshape(depth, ATTN_KV_HEADS, PAIRS, 2).transpose(0, 1, 3, 2)
    t = jnp.repeat(t[..., None], WINDOW, axis=-1)
    return t.reshape(depth, 2 * ATTN_KV_HEADS, PAIRS * WINDOW)


def kernel(x, c, positions, ada_w, ada_b, norm_mix_g, w_in, b_gates, conv_qk, mlstm_head_g, sinks,
           w_out, norm_ffn_g, w_up, conv_ffn, w_down, final_g):
    b, s, d = x.shape
    depth = ada_w.shape[0]
    nh = MLSTM_HEADS

    c_pad = jnp.pad(c, ((0, 2 * SUBLANES - b), (0, 0)))
    mod = _modulation(c_pad, ada_w, ada_b)[:, :b].reshape(depth, b, 6, 1, d)
    cos, sina, sinb = _rope_tables(positions)

    w_in_p = _pack_w_in(w_in)
    w_out_b = w_out.astype(BF16)
    w_up_b = w_up.astype(BF16)
    w_down_b = w_down.astype(BF16)
    bg = jnp.pad(b_gates, ((0, 0), (0, LANES - 2 * nh))).reshape(depth, 1, LANES)
    norm_mix = norm_mix_g.reshape(depth, 1, d)
    norm_ffn = norm_ffn_g.reshape(depth, 1, d)
    head_g = mlstm_head_g.reshape(depth, 1, V_W)
    sink_rows = _sink_rows(sinks)
    gate_sel = _gate_selector()
    final_g2 = final_g.reshape(1, d)

    for l in range(depth):
        qk, v, o_sig, cols, qa, ka, va = _inproj(
            l, x, mod, norm_mix, w_in_p, conv_qk, bg, cos, sina, sinb)
        ym = _mlstm(l, qk, v, o_sig, cols, head_g, gate_sel)
        ya = _attention(l, qa, ka, va, sink_rows)
        x = _outproj(l, ym, ya, x, mod, w_out_b)
        x = _ffn(l, x, mod, norm_ffn, final_g2, w_up_b, conv_ffn, w_down_b,
                 final_norm=(l == depth - 1))
    return x
```

```python
import functools

import jax
import jax.numpy as jnp
from jax import lax
from jax.experimental import pallas as pl
from jax.experimental.pallas import tpu as pltpu

F32 = jnp.float32
BF16 = jnp.bfloat16

MLSTM_HEADS = 4
MLSTM_DQK = 128
MLSTM_DV = 256
MLSTM_CONV = 4
MLSTM_CHUNK = 256
GATE_SOFTCAP = 15.0
ATTN_HEAD_DIM = 64
ATTN_HEADS = 16
ATTN_KV_HEADS = 2
WINDOW = 128
ROPE_DIM = 16
ROPE_THETA = 500000.0
FFN_CONV = 3
FFN_ROW_BLOCKS = 4
NORM_EPS = 1e-6

LANES = 128
SUBLANES = 8
VMEM_LIMIT = 61 * 1024 * 1024

QK_W = 2 * MLSTM_HEADS * MLSTM_DQK
V_W = MLSTM_HEADS * MLSTM_DV
QA_W = ATTN_HEADS * ATTN_HEAD_DIM
KV_W = ATTN_KV_HEADS * ATTN_HEAD_DIM
OFF_QK = 0
OFF_V = OFF_QK + QK_W
OFF_O = OFF_V + V_W
OFF_QA = OFF_O + V_W
OFF_KA = OFF_QA + QA_W
OFF_VA = OFF_KA + KV_W
OFF_G = OFF_VA + KV_W
IN_PACKED = OFF_G + LANES

COL_G = 0
COL_B = MLSTM_HEADS
COL_CG = 2 * MLSTM_HEADS
PAIRS = ATTN_HEADS // ATTN_KV_HEADS // 2


def _params(*sem):
    return pltpu.CompilerParams(dimension_semantics=sem, vmem_limit_bytes=VMEM_LIMIT)


def _pick_tile(n, pref):
    t = min(n, pref)
    while n % t:
        t //= 2
    return t


def _ffn_tiles(s, ff):
    return _pick_tile(s, 1024), _pick_tile(ff, 512)


def _mod_kernel(c_ref, w_ref, b_ref, o_ref):
    ca = jax.nn.silu(c_ref[...]).astype(BF16)
    o_ref[...] = jnp.dot(ca, w_ref[...].astype(BF16), preferred_element_type=F32) + b_ref[...]


def _modulation(c_pad, ada_w, ada_b):
    depth, d, n = ada_w.shape
    rows = c_pad.shape[0]
    tn = _pick_tile(n, 1024)
    return pl.pallas_call(
        _mod_kernel,
        grid=(depth, n // tn),
        in_specs=[
            pl.BlockSpec((rows, d), lambda l, j: (0, 0)),
            pl.BlockSpec((None, d, tn), lambda l, j: (l, 0, j)),
            pl.BlockSpec((None, 1, tn), lambda l, j: (l, 0, j)),
        ],
        out_specs=pl.BlockSpec((None, rows, tn), lambda l, j: (l, 0, j)),
        out_shape=jax.ShapeDtypeStruct((depth, rows, n), F32),
        compiler_params=_params("arbitrary", "arbitrary"),
        name="adaln_mod",
    )(c_pad, ada_w, ada_b.reshape(depth, 1, n))


def _rope_kernel(pos_ref, freq_ref, cos_ref, sina_ref, sinb_ref):
    ang = pos_ref[...].astype(F32) * freq_ref[...]
    d = lax.broadcasted_iota(jnp.int32, ang.shape, 1) % ATTN_HEAD_DIM
    c = jnp.cos(ang)
    s = jnp.sin(ang)
    half = ROPE_DIM // 2
    cos_ref[...] = jnp.where(d < ROPE_DIM, c, 1.0)
    sina_ref[...] = jnp.where(d < half, -s, 0.0)
    sinb_ref[...] = jnp.where((d >= half) & (d < ROPE_DIM), s, 0.0)


def _rope_tables(positions):
    b, s = positions.shape
    ts = _pick_tile(s, 1024)
    inv_freq = ROPE_THETA ** (-jnp.arange(0, ROPE_DIM, 2, dtype=F32) / ROPE_DIM)
    per_head = jnp.concatenate(
        [inv_freq, inv_freq, jnp.zeros((ATTN_HEAD_DIM - ROPE_DIM,), F32)])
    freq = jnp.tile(per_head, LANES // ATTN_HEAD_DIM).reshape(1, LANES)
    tab = jax.ShapeDtypeStruct((b, s, LANES), F32)
    spec = pl.BlockSpec((None, ts, LANES), lambda i, j: (i, j, 0))
    return pl.pallas_call(
        _rope_kernel,
        grid=(b, s // ts),
        in_specs=[pl.BlockSpec((None, ts, 1), lambda i, j: (i, j, 0)),
                  pl.BlockSpec((1, LANES), lambda i, j: (0, 0))],
        out_specs=[spec, spec, spec],
        out_shape=[tab, tab, tab],
        compiler_params=_params("arbitrary", "arbitrary"),
        name="rope_tables",
    )(positions.reshape(b, s, 1), freq)


def _adaln(x, g, shift, scale):
    ms = jnp.mean(x * x, axis=-1, keepdims=True)
    h = x * lax.rsqrt(ms + NORM_EPS) * g
    return h * (1.0 + scale) + shift


def _rope(t, cos, sina, sinb):
    n = t.shape[-1]
    half = ROPE_DIM // 2
    return t * cos + pltpu.roll(t, n - half, 1) * sina + pltpu.roll(t, half, 1) * sinb


def _chunk_scan_rows(x, op, fill):
    pos = lax.broadcasted_iota(jnp.int32, x.shape, 0) % MLSTM_CHUNK
    step = 1
    while step < MLSTM_CHUNK:
        x = op(x, jnp.where(pos >= step, pltpu.roll(x, step, 0), fill))
        step *= 2
    return x


def _gate_columns(gates):
    nh = MLSTM_HEADS
    lane = lax.broadcasted_iota(jnp.int32, gates.shape, 1)
    b = _chunk_scan_rows(jax.nn.log_sigmoid(gates), jnp.add, 0.0)
    g = gates - pltpu.roll(b, LANES - nh, 1)
    cg = _chunk_scan_rows(g, jnp.maximum, -jnp.inf)
    return jnp.where(lane < COL_B, g, jnp.where(lane < COL_CG, b, pltpu.roll(cg, COL_CG, 1)))


def _inproj_kernel(x_ref, shift_ref, scale_ref, g_ref, w_ref, cw_ref, bg_ref,
                   cos_ref, sina_ref, sinb_ref,
                   qk_ref, v_ref, o_ref, col_ref, qa_ref, ka_ref, va_ref,
                   hb_ref, cbuf_ref):
    tm = x_ref.shape[0]
    halo = SUBLANES

    @pl.when(pl.program_id(1) == 0)
    def _():
        cbuf_ref[...] = jnp.zeros(cbuf_ref.shape, F32)

    hb_ref[...] = _adaln(x_ref[...], g_ref[...], shift_ref[...], scale_ref[...]).astype(BF16)

    def proj(off, width):
        return jnp.dot(hb_ref[...], w_ref[:, off:off + width], preferred_element_type=F32)

    pg = proj(OFF_G, LANES) + bg_ref[...]
    col_ref[...] = _gate_columns(GATE_SOFTCAP * jnp.tanh(pg / GATE_SOFTCAP))

    cw = cw_ref[...]
    n_parts = 4
    part = QK_W // n_parts
    for c in range(n_parts):
        lo = c * part
        pq = proj(OFF_QK + lo, part)
        prev = cbuf_ref[:, lo:lo + part]
        acc = _shift_rows(pq, prev, 3) * cw[0:1, lo:lo + part]
        acc = acc + _shift_rows(pq, prev, 2) * cw[1:2, lo:lo + part]
        acc = acc + _shift_rows(pq, prev, 1) * cw[2:3, lo:lo + part]
        acc = acc + pq * cw[3:4, lo:lo + part]
        cbuf_ref[:, lo:lo + part] = pq[tm - halo:tm, :]
        act = jax.nn.silu(acc)
        if lo < QK_W // 2:
            act = act * (MLSTM_DQK ** -0.5)
        qk_ref[:, lo:lo + part] = act.astype(BF16)
        o_ref[:, lo:lo + part] = jax.nn.sigmoid(proj(OFF_O + lo, part))

    cos = cos_ref[...]
    sina = sina_ref[...]
    sinb = sinb_ref[...]
    for c in range(QA_W // 256):
        pa = proj(OFF_QA + c * 256, 256)
        for k in range(2):
            lo = c * 256 + k * LANES
            r = _rope(pa[:, k * LANES:(k + 1) * LANES], cos, sina, sinb)
            qa_ref[:, lo:lo + LANES] = (r * (ATTN_HEAD_DIM ** -0.5)).astype(BF16)

    pkv = proj(OFF_KA, 2 * KV_W)
    ka_ref[...] = _rope(pkv[:, 0:KV_W], cos, sina, sinb)
    va_ref[...] = pkv[:, KV_W:2 * KV_W]

    for c in range(2):
        lo = c * (V_W // 2)
        v_ref[:, lo:lo + V_W // 2] = proj(OFF_V + lo, V_W // 2).astype(BF16)


def _inproj(l, x, mod, g, w_packed, conv_qk, bg, cos, sina, sinb):
    b, s, d = x.shape
    tm = _pick_tile(s, 512)
    row = lambda i, j: (i, j, 0)
    lay = lambda i, j: (l, 0, 0)
    outs = [
        (QK_W, BF16), (V_W, BF16), (V_W, F32), (LANES, F32), (QA_W, BF16), (KV_W, F32), (KV_W, F32),
    ]
    return pl.pallas_call(
        _inproj_kernel,
        grid=(b, s // tm),
        in_specs=[
            pl.BlockSpec((None, tm, d), row),
            pl.BlockSpec((None, None, None, 1, d), lambda i, j: (l, i, 0, 0, 0)),
            pl.BlockSpec((None, None, None, 1, d), lambda i, j: (l, i, 1, 0, 0)),
            pl.BlockSpec((None, 1, d), lay),
            pl.BlockSpec((None, d, IN_PACKED), lay),
            pl.BlockSpec((None, MLSTM_CONV, QK_W), lay),
            pl.BlockSpec((None, 1, LANES), lay),
            pl.BlockSpec((None, tm, LANES), row),
            pl.BlockSpec((None, tm, LANES), row),
            pl.BlockSpec((None, tm, LANES), row),
        ],
        out_specs=[pl.BlockSpec((None, tm, w), row) for w, _ in outs],
        out_shape=[jax.ShapeDtypeStruct((b, s, w), dt) for w, dt in outs],
        scratch_shapes=[pltpu.VMEM((tm, d), BF16), pltpu.VMEM((SUBLANES, QK_W), F32)],
        compiler_params=_params("arbitrary", "arbitrary"),
        name="in_proj",
    )(x, mod, mod, g, w_packed, conv_qk, bg, cos, sina, sinb)


def _gate_selector():
    nh = MLSTM_HEADS
    src = jnp.arange(LANES)[:, None]
    group = jnp.arange(3 * nh * LANES)[None, :] // LANES
    base = jnp.asarray([COL_CG, COL_G, COL_B])[group % 3]
    return (src == base + group // 3).astype(BF16)


def _mlstm_kernel(qk_ref, v_ref, o_ref, col_ref, hg_ref, sel_ref, y_ref, c_ref, m_ref):
    L = MLSTM_CHUNK
    nh = MLSTM_HEADS
    n_chunks = qk_ref.shape[0] // L

    @pl.when(pl.program_id(1) == 0)
    def _():
        c_ref[...] = jnp.zeros(c_ref.shape, F32)
        m_ref[...] = jnp.zeros(m_ref.shape, F32)

    ri = lax.broadcasted_iota(jnp.int32, (L, L), 0)
    ci = lax.broadcasted_iota(jnp.int32, (L, L), 1)
    causal = ci <= ri
    ones_col = (lax.broadcasted_iota(jnp.int32, (L, LANES), 1) == 0).astype(BF16)
    nt = (((1,), (1,)), ((), ()))
    tn = (((0,), (0,)), ((), ()))
    units = [(c, h) for c in range(n_chunks) for h in range(nh)]
    rows = [slice(c * L, (c + 1) * L) for c in range(n_chunks)]
    dvs = [slice(h * MLSTM_DV, (h + 1) * MLSTM_DV) for h in range(nh)]

    col = [col_ref[r, :] for r in rows]
    col_t = [cl.T for cl in col]
    q = {(c, h): qk_ref[rows[c], h * MLSTM_DQK:(h + 1) * MLSTM_DQK] for c, h in units}
    k = {(c, h): qk_ref[rows[c], (nh + h) * MLSTM_DQK:(nh + h + 1) * MLSTM_DQK] for c, h in units}
    v_ext = {(c, h): jnp.concatenate([v_ref[rows[c], dvs[h]], ones_col], axis=1) for c, h in units}

    def replicate(x):
        hi = x.astype(BF16)
        rem = x - hi.astype(F32)
        mid = rem.astype(BF16)
        lo = (rem - mid.astype(F32)).astype(BF16)
        sel = sel_ref[...]
        return ((jnp.dot(hi, sel, preferred_element_type=F32)
                 + jnp.dot(mid, sel, preferred_element_type=F32))
                + jnp.dot(lo, sel, preferred_element_type=F32))

    def twice(x):
        return jnp.concatenate([x, x], axis=1)

    rep = [replicate(cl) for cl in col]

    def gate(c, h, kind):
        lo = (3 * h + kind) * LANES
        return rep[c][:, lo:lo + LANES]

    mm, w_inter, decay, w_state = {}, {}, {}, {}
    for h in range(nh):
        m = m_ref[h:h + 1, :]
        for c in range(n_chunks):
            mm[c, h] = jnp.maximum(m, gate(c, h, 0))
            mm_last = mm[c, h][L - 1:L, :]
            w_inter[c, h] = jnp.exp(m - mm[c, h])
            decay[c, h] = jnp.exp(m - mm_last)
            w_state[c, h] = jnp.exp(gate(c, h, 1) - mm_last)
            m = gate(c, h, 2)[L - 1:L, :] + mm_last
        m_ref[h:h + 1, :] = m

    s = {}
    for u in units:
        c, h = u
        g_row = col_t[c][COL_G + h:COL_G + h + 1, :]
        w_intra = jnp.exp(jnp.where(causal, g_row - twice(mm[u]), -jnp.inf))
        s[u] = lax.dot_general(q[u], k[u], nt, preferred_element_type=F32) * w_intra
    intra = {u: jnp.dot(s[u].astype(BF16), v_ext[u][:, 0:MLSTM_DV], preferred_element_type=F32)
             for u in units}
    upd = {u: lax.dot_general((k[u].astype(F32) * w_state[u]).astype(BF16), v_ext[u], tn,
                              preferred_element_type=F32) for u in units}

    c_in = {}
    for h in range(nh):
        state = c_ref[h]
        for c in range(n_chunks):
            c_in[c, h] = state.astype(BF16)
            d = decay[c, h]
            state = jnp.concatenate([d, d, d], axis=1) * state + upd[c, h]
        c_ref[h] = state

    def lane_sum(x):
        return jnp.sum(x[:, 0:LANES] + x[:, LANES:2 * LANES], axis=-1, keepdims=True)

    for u in units:
        c, h = u
        inter = jnp.dot(q[u], c_in[u], preferred_element_type=F32)
        num = intra[u] + twice(w_inter[u]) * inter[:, 0:MLSTM_DV]
        nq = lane_sum(s[u]) + w_inter[u][:, 0:1] * inter[:, MLSTM_DV:MLSTM_DV + 1]
        m_t = gate(c, h, 2)[:, 0:1] + mm[u][:, 0:1]
        rden = 1.0 / jnp.maximum(jnp.abs(nq), jnp.exp(-m_t))
        msq = lane_sum(num * num) * (1.0 / MLSTM_DV)
        scale = rden * lax.rsqrt(rden * rden * msq + NORM_EPS)
        hn = num * scale * hg_ref[:, dvs[h]]
        y_ref[rows[c], dvs[h]] = (o_ref[rows[c], dvs[h]] * hn).astype(BF16)


def _mlstm(l, qk, v, o_sig, cols, head_g, sel):
    b, s, _ = qk.shape
    assert s % MLSTM_CHUNK == 0, (s, MLSTM_CHUNK)
    t = _pick_tile(s // MLSTM_CHUNK, 4) * MLSTM_CHUNK
    row = lambda i, j: (i, j, 0)
    return pl.pallas_call(
        _mlstm_kernel,
        grid=(b, s // t),
        in_specs=[
            pl.BlockSpec((None, t, QK_W), row),
            pl.BlockSpec((None, t, V_W), row),
            pl.BlockSpec((None, t, V_W), row),
            pl.BlockSpec((None, t, LANES), row),
            pl.BlockSpec((None, 1, V_W), lambda i, j: (l, 0, 0)),
            pl.BlockSpec(sel.shape, lambda i, j: (0, 0)),
        ],
        out_specs=pl.BlockSpec((None, t, V_W), row),
        out_shape=jax.ShapeDtypeStruct((b, s, V_W), BF16),
        scratch_shapes=[pltpu.VMEM((MLSTM_HEADS, MLSTM_DQK, MLSTM_DV + LANES), F32),
                        pltpu.VMEM((SUBLANES, LANES), F32)],
        compiler_params=_params("arbitrary", "arbitrary"),
        name="mlstm",
    )(qk, v, o_sig, cols, head_g, sel)


def _block_diag(band, kv_head):
    lane = lax.broadcasted_iota(jnp.int32, band.shape, 1)
    low = lane < ATTN_HEAD_DIM
    rolled = pltpu.roll(band, ATTN_HEAD_DIM, 1)
    if kv_head == 0:
        top = jnp.where(low, band, 0.0)
        bot = jnp.where(low, 0.0, rolled)
    else:
        top = jnp.where(low, rolled, 0.0)
        bot = jnp.where(low, 0.0, band)
    return jnp.concatenate([top, bot], axis=0).astype(BF16)


def _attn_kernel(q_ref, kc_ref, kp_ref, vc_ref, vp_ref, sink_ref, y_ref):
    W = WINDOW
    n_blocks = q_ref.shape[0] // W
    prev_penalty = jnp.where(pl.program_id(1) == 0, -jnp.inf, 0.0)

    key = lax.broadcasted_iota(jnp.int32, (W, PAIRS * W), 0)
    qry = lax.broadcasted_iota(jnp.int32, (W, PAIRS * W), 1) % W
    use_prev = key > qry

    def softmax_t(prev, cur, sink):
        comb = jnp.where(use_prev, prev, cur)
        m = jnp.maximum(jnp.max(comb, axis=0, keepdims=True), sink)
        p = jnp.exp(comb - m)
        den = jnp.sum(p, axis=0, keepdims=True) + jnp.exp(sink - m)
        p = p * (1.0 / den)
        return jnp.where(use_prev, p, 0.0), jnp.where(use_prev, 0.0, p)

    for blk in range(n_blocks):
        rows = slice(blk * W, (blk + 1) * W)
        if blk == 0:
            k_prev, v_prev = kp_ref[...], vp_ref[...]
        else:
            prev = slice((blk - 1) * W, blk * W)
            k_prev, v_prev = kc_ref[prev, :], vc_ref[prev, :]
        k_band = jnp.concatenate([k_prev, kc_ref[rows, :]], axis=0)
        v_band = jnp.concatenate([v_prev, vc_ref[rows, :]], axis=0)
        for kvh in range(ATTN_KV_HEADS):
            k_bd = _block_diag(k_band, kvh)
            v_bd = _block_diag(v_band, kvh)
            lanes = [slice((kvh * PAIRS + p) * LANES, (kvh * PAIRS + p + 1) * LANES)
                     for p in range(PAIRS)]
            q_all = jnp.concatenate([q_ref[rows, ln] for ln in lanes], axis=0)
            st = lax.dot_general(k_bd, q_all, (((1,), (1,)), ((), ())),
                                 preferred_element_type=F32)
            prev_e, cur_e, prev_o, cur_o = (st[i * W:(i + 1) * W, :] for i in range(4))
            if blk == 0:
                prev_e = prev_e + prev_penalty
                prev_o = prev_o + prev_penalty
            pe_prev, pe_cur = softmax_t(prev_e, cur_e, sink_ref[2 * kvh:2 * kvh + 1, :])
            po_prev, po_cur = softmax_t(prev_o, cur_o, sink_ref[2 * kvh + 1:2 * kvh + 2, :])
            pt = jnp.concatenate([pe_prev, pe_cur, po_prev, po_cur], axis=0).astype(BF16)
            out_t = lax.dot_general(v_bd, pt, (((0,), (0,)), ((), ())),
                                    preferred_element_type=F32)
            for p in range(PAIRS):
                y_ref[rows, lanes[p]] = out_t[:, p * W:(p + 1) * W].T.astype(BF16)


def _attention(l, qa, ka, va, sink_rows):
    b, s, _ = qa.shape
    tq = _pick_tile(s, 1024)
    per = tq // WINDOW
    cur = lambda i, j: (i, j, 0)
    prev = lambda i, j: (i, jnp.maximum(j * per - 1, 0), 0)
    return pl.pallas_call(
        _attn_kernel,
        grid=(b, s // tq),
        in_specs=[
            pl.BlockSpec((None, tq, QA_W), cur),
            pl.BlockSpec((None, tq, KV_W), cur),
            pl.BlockSpec((None, WINDOW, KV_W), prev),
            pl.BlockSpec((None, tq, KV_W), cur),
            pl.BlockSpec((None, WINDOW, KV_W), prev),
            pl.BlockSpec((None, 2 * ATTN_KV_HEADS, PAIRS * WINDOW), lambda i, j: (l, 0, 0)),
        ],
        out_specs=pl.BlockSpec((None, tq, QA_W), cur),
        out_shape=jax.ShapeDtypeStruct((b, s, QA_W), BF16),
        compiler_params=_params("arbitrary", "arbitrary"),
        name="swa_attention",
    )(qa, ka, ka, va, va, sink_rows)


def _outproj_kernel(ym_ref, ya_ref, x_ref, gate_ref, w_ref, o_ref):
    km = ym_ref.shape[1]
    y = jnp.dot(ym_ref[...], w_ref[0:km, :], preferred_element_type=F32)
    y = y + jnp.dot(ya_ref[...], w_ref[km:, :], preferred_element_type=F32)
    o_ref[...] = x_ref[...] + gate_ref[...] * y


def _outproj(l, ym, ya, x, mod, w_out):
    b, s, d = x.shape
    tm = _pick_tile(s, 512)
    row = lambda i, j: (i, j, 0)
    return pl.pallas_call(
        _outproj_kernel,
        grid=(b, s // tm),
        in_specs=[
            pl.BlockSpec((None, tm, ym.shape[2]), row),
            pl.BlockSpec((None, tm, ya.shape[2]), row),
            pl.BlockSpec((None, tm, d), row),
            pl.BlockSpec((None, None, None, 1, d), lambda i, j: (l, i, 2, 0, 0)),
            pl.BlockSpec((None,) + w_out.shape[1:], lambda i, j: (l, 0, 0)),
        ],
        out_specs=pl.BlockSpec((None, tm, d), row),
        out_shape=jax.ShapeDtypeStruct((b, s, d), F32),
        compiler_params=_params("arbitrary", "arbitrary"),
        name="out_proj",
    )(ym, ya, x, mod, w_out)


def _shift_rows(x, prev_tail, k):
    rolled = pltpu.roll(x, k, 0)
    sub = lax.broadcasted_iota(jnp.int32, prev_tail.shape, 0)
    head = jnp.where(sub < k, pltpu.roll(prev_tail, k, 0), rolled[0:SUBLANES, :])
    return jnp.concatenate([head, rolled[SUBLANES:, :]], axis=0)


def _ffn_kernel(x_ref, shift_ref, scale_ref, gate_ref, g_ref, fg_ref, wg_ref, wu_ref, cw_ref, wd_ref,
                o_ref, hb_ref, gs_ref, us_ref, halo_ref, tail_ref, *, nf, final_norm):
    tm = x_ref.shape[0]
    halo = SUBLANES
    i = pl.program_id(1)
    j = pl.program_id(2)

    def up_g():
        gs_ref[...] = jnp.dot(hb_ref[...], wg_ref[...], preferred_element_type=F32)

    def up_u():
        us_ref[...] = jnp.dot(hb_ref[...], wu_ref[...], preferred_element_type=F32)

    def fill_halo():
        @pl.when(i == 0)
        def _():
            halo_ref[...] = jnp.zeros(halo_ref.shape, F32)

        @pl.when(i > 0)
        def _():
            halo_ref[...] = tail_ref[j - 1]

    n_rb = FFN_ROW_BLOCKS if tm % (FFN_ROW_BLOCKS * SUBLANES) == 0 else 1
    rb = tm // n_rb

    def gated_down():
        cw = cw_ref[...] * 0.5
        for r in range(n_rb):
            rows = slice(r * rb, (r + 1) * rb)
            g = gs_ref[rows, :]
            prev = halo_ref[...] if r == 0 else gs_ref[r * rb - halo:r * rb, :]
            half_c = _shift_rows(g, prev, 2) * cw[0:1, :]
            half_c = half_c + _shift_rows(g, prev, 1) * cw[1:2, :]
            half_c = half_c + g * cw[2:3, :]
            act = (half_c * (1.0 + jnp.tanh(half_c)) * us_ref[rows, :]).astype(BF16)
            o_ref[rows, :] += jnp.dot(act, wd_ref[...], preferred_element_type=F32)
        tail_ref[j - 1] = gs_ref[tm - halo:tm, :]

    @pl.when(j == 0)
    def _():
        hb_ref[...] = _adaln(x_ref[...], g_ref[...], shift_ref[...], scale_ref[...]).astype(BF16)
        o_ref[...] = jnp.zeros(o_ref.shape, F32)
        up_g()
        up_u()

    @pl.when((j > 0) & (j < nf))
    def _():
        fill_halo()
        gated_down()
        up_g()
        up_u()

    @pl.when(j == nf)
    def _():
        fill_halo()
        gated_down()
        res = x_ref[...] + gate_ref[...] * o_ref[...]
        if final_norm:
            ms = jnp.mean(res * res, axis=-1, keepdims=True)
            res = res * lax.rsqrt(ms + NORM_EPS) * fg_ref[...]
        o_ref[...] = res


def _ffn(l, x, mod, g, final_g, w_up, conv_ffn, w_down, final_norm):
    b, s, d = x.shape
    ff = w_down.shape[1]
    tm, tf = _ffn_tiles(s, ff)
    nf = ff // tf
    row = lambda i, t, j: (i, t, 0)
    mod_spec = lambda k: pl.BlockSpec((None, None, None, 1, d), lambda i, t, j: (l, i, k, 0, 0))
    up_tile = lambda j: jnp.minimum(j, nf - 1)
    down_tile = lambda j: jnp.maximum(j - 1, 0)
    return pl.pallas_call(
        functools.partial(_ffn_kernel, nf=nf, final_norm=final_norm),
        grid=(b, s // tm, nf + 1),
        in_specs=[
            pl.BlockSpec((None, tm, d), row),
            mod_spec(3), mod_spec(4), mod_spec(5),
            pl.BlockSpec((None, 1, d), lambda i, t, j: (l, 0, 0)),
            pl.BlockSpec((1, d), lambda i, t, j: (0, 0)),
            pl.BlockSpec((None, d, tf), lambda i, t, j: (l, 0, up_tile(j))),
            pl.BlockSpec((None, d, tf), lambda i, t, j: (l, 0, nf + up_tile(j))),
            pl.BlockSpec((None, FFN_CONV, tf), lambda i, t, j: (l, 0, down_tile(j))),
            pl.BlockSpec((None, tf, d), lambda i, t, j: (l, down_tile(j), 0)),
        ],
        out_specs=pl.BlockSpec((None, tm, d), row),
        out_shape=jax.ShapeDtypeStruct((b, s, d), F32),
        scratch_shapes=[pltpu.VMEM((tm, d), BF16),
                        pltpu.VMEM((tm, tf), F32),
                        pltpu.VMEM((tm, tf), F32),
                        pltpu.VMEM((SUBLANES, tf), F32),
                        pltpu.VMEM((nf, SUBLANES, tf), F32)],
        compiler_params=_params("arbitrary", "arbitrary", "arbitrary"),
        name="convglu_ffn",
    )(x, mod, mod, mod, g, final_g, w_up, w_up, conv_ffn, w_down)


def _pack_w_in(w_in):
    nh = MLSTM_HEADS
    o_gates = QK_W + 2 * V_W
    o_attn = o_gates + 2 * nh
    gates = w_in[:, :, o_gates:o_attn]
    gates = jnp.pad(gates, ((0, 0), (0, 0), (0, LANES - 2 * nh)))
    packed = jnp.concatenate([w_in[:, :, :o_gates], w_in[:, :, o_attn:], gates], axis=-1)
    return packed.astype(BF16)


def _sink_rows(sinks):
    depth = sinks.shape[0]
    t = sinks.reshape(depth, ATTN_KV_HEADS, PAIRS, 2).transpose(0, 1, 3, 2)
    t = jnp.repeat(t[..., None], WINDOW, axis=-1)
    return t.reshape(depth, 2 * ATTN_KV_HEADS, PAIRS * WINDOW)


def kernel(x, c, positions, ada_w, ada_b, norm_mix_g, w_in, b_gates, conv_qk, mlstm_head_g, sinks,
           w_out, norm_ffn_g, w_up, conv_ffn, w_down, final_g):
    b, s, d = x.shape
    depth = ada_w.shape[0]
    nh = MLSTM_HEADS

    c_pad = jnp.pad(c, ((0, 2 * SUBLANES - b), (0, 0)))
    mod = _modulation(c_pad, ada_w, ada_b)[:, :b].reshape(depth, b, 6, 1, d)
    cos, sina, sinb = _rope_tables(positions)

    w_in_p = _pack_w_in(w_in)
    w_out_b = w_out.astype(BF16)
    w_up_b = w_up.astype(BF16)
    w_down_b = w_down.astype(BF16)
    bg = jnp.pad(b_gates, ((0, 0), (0, LANES - 2 * nh))).reshape(depth, 1, LANES)
    norm_mix = norm_mix_g.reshape(depth, 1, d)
    norm_ffn = norm_ffn_g.reshape(depth, 1, d)
    head_g = mlstm_head_g.reshape(depth, 1, V_W)
    sink_rows = _sink_rows(sinks)
    gate_sel = _gate_selector()
    final_g2 = final_g.reshape(1, d)

    for l in range(depth):
        qk, v, o_sig, cols, qa, ka, va = _inproj(
            l, x, mod, norm_mix, w_in_p, conv_qk, bg, cos, sina, sinb)
        ym = _mlstm(l, qk, v, o_sig, cols, head_g, gate_sel)
        ya = _attention(l, qa, ka, va, sink_rows)
        x = _outproj(l, ym, ya, x, mod, w_out_b)
        x = _ffn(l, x, mod, norm_ffn, final_g2, w_up_b, conv_ffn, w_down_b,
                 final_norm=(l == depth - 1))
    return x
```

```python
import functools

import jax
import jax.numpy as jnp
from jax import lax
from jax.experimental import pallas as pl
from jax.experimental.pallas import tpu as pltpu

F32 = jnp.float32
BF16 = jnp.bfloat16

MLSTM_HEADS = 4
MLSTM_DQK = 128
MLSTM_DV = 256
MLSTM_CONV = 4
MLSTM_CHUNK = 256
GATE_SOFTCAP = 15.0
ATTN_HEAD_DIM = 64
ATTN_HEADS = 16
ATTN_KV_HEADS = 2
WINDOW = 128
ROPE_DIM = 16
ROPE_THETA = 500000.0
FFN_CONV = 3
FFN_ROW_BLOCKS = 4
NORM_EPS = 1e-6

LANES = 128
SUBLANES = 8
MXU_WIDTH = 256
VMEM_LIMIT = 61 * 1024 * 1024

TM_PROJ = 512
TQ_ATTN = 1024
MLSTM_STEP_CHUNKS = 4
TM_FFN = 1024
TF_FFN = 512
TN_MOD = 1024
TS_ROPE = 1024

QK_W = 2 * MLSTM_HEADS * MLSTM_DQK
V_W = MLSTM_HEADS * MLSTM_DV
QA_W = ATTN_HEADS * ATTN_HEAD_DIM
KV_W = ATTN_KV_HEADS * ATTN_HEAD_DIM
OFF_QK = 0
OFF_V = OFF_QK + QK_W
OFF_O = OFF_V + V_W
OFF_QA = OFF_O + V_W
OFF_KA = OFF_QA + QA_W
OFF_VA = OFF_KA + KV_W
OFF_G = OFF_VA + KV_W
IN_PACKED = OFF_G + LANES

COL_G = 0
COL_B = MLSTM_HEADS
COL_CG = 2 * MLSTM_HEADS
PAIRS = ATTN_HEADS // ATTN_KV_HEADS // 2


def _params(*sem):
    return pltpu.CompilerParams(dimension_semantics=sem, vmem_limit_bytes=VMEM_LIMIT)


def _pick_tile(n, pref):
    t = min(n, pref)
    while n % t:
        t //= 2
    return t


def _ffn_tiles(s, ff):
    return _pick_tile(s, TM_FFN), _pick_tile(ff, TF_FFN)


def _mod_kernel(c_ref, w_ref, b_ref, o_ref):
    ca = jax.nn.silu(c_ref[...]).astype(BF16)
    o_ref[...] = jnp.dot(ca, w_ref[...].astype(BF16), preferred_element_type=F32) + b_ref[...]


def _modulation(c_pad, ada_w, ada_b):
    depth, d, n = ada_w.shape
    rows = c_pad.shape[0]
    tn = _pick_tile(n, TN_MOD)
    return pl.pallas_call(
        _mod_kernel,
        grid=(depth, n // tn),
        in_specs=[
            pl.BlockSpec((rows, d), lambda l, j: (0, 0)),
            pl.BlockSpec((None, d, tn), lambda l, j: (l, 0, j)),
            pl.BlockSpec((None, 1, tn), lambda l, j: (l, 0, j)),
        ],
        out_specs=pl.BlockSpec((None, rows, tn), lambda l, j: (l, 0, j)),
        out_shape=jax.ShapeDtypeStruct((depth, rows, n), F32),
        compiler_params=_params("arbitrary", "arbitrary"),
        name="adaln_mod",
    )(c_pad, ada_w, ada_b.reshape(depth, 1, n))


def _rope_kernel(pos_ref, freq_ref, cos_ref, sina_ref, sinb_ref):
    ang = pos_ref[...].astype(F32) * freq_ref[...]
    d = lax.broadcasted_iota(jnp.int32, ang.shape, 1) % ATTN_HEAD_DIM
    c = jnp.cos(ang)
    s = jnp.sin(ang)
    half = ROPE_DIM // 2
    cos_ref[...] = jnp.where(d < ROPE_DIM, c, 1.0)
    sina_ref[...] = jnp.where(d < half, -s, 0.0)
    sinb_ref[...] = jnp.where((d >= half) & (d < ROPE_DIM), s, 0.0)


def _rope_tables(positions):
    b, s = positions.shape
    ts = _pick_tile(s, TS_ROPE)
    inv_freq = ROPE_THETA ** (-jnp.arange(0, ROPE_DIM, 2, dtype=F32) / ROPE_DIM)
    per_head = jnp.concatenate(
        [inv_freq, inv_freq, jnp.zeros((ATTN_HEAD_DIM - ROPE_DIM,), F32)])
    freq = jnp.tile(per_head, LANES // ATTN_HEAD_DIM).reshape(1, LANES)
    tab = jax.ShapeDtypeStruct((b, s, LANES), F32)
    spec = pl.BlockSpec((None, ts, LANES), lambda i, j: (i, j, 0))
    return pl.pallas_call(
        _rope_kernel,
        grid=(b, s // ts),
        in_specs=[pl.BlockSpec((None, ts, 1), lambda i, j: (i, j, 0)),
                  pl.BlockSpec((1, LANES), lambda i, j: (0, 0))],
        out_specs=[spec, spec, spec],
        out_shape=[tab, tab, tab],
        compiler_params=_params("arbitrary", "arbitrary"),
        name="rope_tables",
    )(positions.reshape(b, s, 1), freq)


def _adaln(x, g, shift, scale):
    ms = jnp.mean(x * x, axis=-1, keepdims=True)
    h = x * lax.rsqrt(ms + NORM_EPS) * g
    return h * (1.0 + scale) + shift


def _rope(t, cos, sina, sinb):
    n = t.shape[-1]
    half = ROPE_DIM // 2
    return t * cos + pltpu.roll(t, n - half, 1) * sina + pltpu.roll(t, half, 1) * sinb


def _chunk_scan_rows(x, op, fill):
    pos = lax.broadcasted_iota(jnp.int32, x.shape, 0) % MLSTM_CHUNK
    step = 1
    while step < MLSTM_CHUNK:
        x = op(x, jnp.where(pos >= step, pltpu.roll(x, step, 0), fill))
        step *= 2
    return x


def _gate_columns(gates):
    nh = MLSTM_HEADS
    lane = lax.broadcasted_iota(jnp.int32, gates.shape, 1)
    b = _chunk_scan_rows(jax.nn.log_sigmoid(gates), jnp.add, 0.0)
    g = gates - pltpu.roll(b, LANES - nh, 1)
    cg = _chunk_scan_rows(g, jnp.maximum, -jnp.inf)
    return jnp.where(lane < COL_B, g, jnp.where(lane < COL_CG, b, pltpu.roll(cg, COL_CG, 1)))


def _inproj_kernel(x_ref, shift_ref, scale_ref, g_ref, w_ref, cw_ref, bg_ref,
                   cos_ref, sina_ref, sinb_ref,
                   qk_ref, v_ref, o_ref, col_ref, qa_ref, ka_ref, va_ref,
                   hb_ref, cbuf_ref):
    tm = x_ref.shape[0]
    halo = SUBLANES

    @pl.when(pl.program_id(1) == 0)
    def _():
        cbuf_ref[...] = jnp.zeros(cbuf_ref.shape, F32)

    hb_ref[...] = _adaln(x_ref[...], g_ref[...], shift_ref[...], scale_ref[...]).astype(BF16)

    def proj(off, width):
        return jnp.dot(hb_ref[...], w_ref[:, off:off + width], preferred_element_type=F32)

    pg = proj(OFF_G, LANES) + bg_ref[...]
    col_ref[...] = _gate_columns(GATE_SOFTCAP * jnp.tanh(pg / GATE_SOFTCAP))

    cw = cw_ref[...]
    n_parts = 4
    part = QK_W // n_parts
    for c in range(n_parts):
        lo = c * part
        pq = proj(OFF_QK + lo, part)
        prev = cbuf_ref[:, lo:lo + part]
        acc = _shift_rows(pq, prev, 3) * cw[0:1, lo:lo + part]
        acc = acc + _shift_rows(pq, prev, 2) * cw[1:2, lo:lo + part]
        acc = acc + _shift_rows(pq, prev, 1) * cw[2:3, lo:lo + part]
        acc = acc + pq * cw[3:4, lo:lo + part]
        cbuf_ref[:, lo:lo + part] = pq[tm - halo:tm, :]
        act = jax.nn.silu(acc)
        if lo < QK_W // 2:
            act = act * (MLSTM_DQK ** -0.5)
        qk_ref[:, lo:lo + part] = act.astype(BF16)
        o_ref[:, lo:lo + part] = jax.nn.sigmoid(proj(OFF_O + lo, part))

    cos = cos_ref[...]
    sina = sina_ref[...]
    sinb = sinb_ref[...]
    for c in range(QA_W // MXU_WIDTH):
        pa = proj(OFF_QA + c * MXU_WIDTH, MXU_WIDTH)
        for k in range(MXU_WIDTH // LANES):
            lo = c * MXU_WIDTH + k * LANES
            r = _rope(pa[:, k * LANES:(k + 1) * LANES], cos, sina, sinb)
            qa_ref[:, lo:lo + LANES] = (r * (ATTN_HEAD_DIM ** -0.5)).astype(BF16)

    pkv = proj(OFF_KA, 2 * KV_W)
    ka_ref[...] = _rope(pkv[:, 0:KV_W], cos, sina, sinb)
    va_ref[...] = pkv[:, KV_W:2 * KV_W]

    for c in range(2):
        lo = c * (V_W // 2)
        v_ref[:, lo:lo + V_W // 2] = proj(OFF_V + lo, V_W // 2).astype(BF16)


def _inproj(l, x, mod, g, w_packed, conv_qk, bg, cos, sina, sinb):
    b, s, d = x.shape
    tm = _pick_tile(s, TM_PROJ)
    row = lambda i, j: (i, j, 0)
    lay = lambda i, j: (l, 0, 0)
    outs = [
        (QK_W, BF16), (V_W, BF16), (V_W, F32), (LANES, F32), (QA_W, BF16), (KV_W, F32), (KV_W, F32),
    ]
    return pl.pallas_call(
        _inproj_kernel,
        grid=(b, s // tm),
        in_specs=[
            pl.BlockSpec((None, tm, d), row),
            pl.BlockSpec((None, None, None, 1, d), lambda i, j: (l, i, 0, 0, 0)),
            pl.BlockSpec((None, None, None, 1, d), lambda i, j: (l, i, 1, 0, 0)),
            pl.BlockSpec((None, 1, d), lay),
            pl.BlockSpec((None, d, IN_PACKED), lay),
            pl.BlockSpec((None, MLSTM_CONV, QK_W), lay),
            pl.BlockSpec((None, 1, LANES), lay),
            pl.BlockSpec((None, tm, LANES), row),
            pl.BlockSpec((None, tm, LANES), row),
            pl.BlockSpec((None, tm, LANES), row),
        ],
        out_specs=[pl.BlockSpec((None, tm, w), row) for w, _ in outs],
        out_shape=[jax.ShapeDtypeStruct((b, s, w), dt) for w, dt in outs],
        scratch_shapes=[pltpu.VMEM((tm, d), BF16), pltpu.VMEM((SUBLANES, QK_W), F32)],
        compiler_params=_params("arbitrary", "arbitrary"),
        name="in_proj",
    )(x, mod, mod, g, w_packed, conv_qk, bg, cos, sina, sinb)


def _gate_selector():
    nh = MLSTM_HEADS
    src = jnp.arange(LANES)[:, None]
    group = jnp.arange(3 * nh * LANES)[None, :] // LANES
    base = jnp.asarray([COL_CG, COL_G, COL_B])[group % 3]
    return (src == base + group // 3).astype(BF16)


def _mlstm_kernel(qk_ref, v_ref, o_ref, col_ref, hg_ref, sel_ref, y_ref, c_ref, m_ref):
    L = MLSTM_CHUNK
    nh = MLSTM_HEADS
    n_chunks = qk_ref.shape[0] // L

    @pl.when(pl.program_id(1) == 0)
    def _():
        c_ref[...] = jnp.zeros(c_ref.shape, F32)
        m_ref[...] = jnp.zeros(m_ref.shape, F32)

    ri = lax.broadcasted_iota(jnp.int32, (L, L), 0)
    ci = lax.broadcasted_iota(jnp.int32, (L, L), 1)
    causal = ci <= ri
    ones_col = (lax.broadcasted_iota(jnp.int32, (L, LANES), 1) == 0).astype(BF16)
    nt = (((1,), (1,)), ((), ()))
    tn = (((0,), (0,)), ((), ()))
    units = [(c, h) for c in range(n_chunks) for h in range(nh)]
    rows = [slice(c * L, (c + 1) * L) for c in range(n_chunks)]
    dvs = [slice(h * MLSTM_DV, (h + 1) * MLSTM_DV) for h in range(nh)]

    col = [col_ref[r, :] for r in rows]
    col_t = [cl.T for cl in col]
    q = {(c, h): qk_ref[rows[c], h * MLSTM_DQK:(h + 1) * MLSTM_DQK] for c, h in units}
    k = {(c, h): qk_ref[rows[c], (nh + h) * MLSTM_DQK:(nh + h + 1) * MLSTM_DQK] for c, h in units}
    v_ext = {(c, h): jnp.concatenate([v_ref[rows[c], dvs[h]], ones_col], axis=1) for c, h in units}

    def replicate(x):
        hi = x.astype(BF16)
        rem = x - hi.astype(F32)
        mid = rem.astype(BF16)
        lo = (rem - mid.astype(F32)).astype(BF16)
        sel = sel_ref[...]
        return ((jnp.dot(hi, sel, preferred_element_type=F32)
                 + jnp.dot(mid, sel, preferred_element_type=F32))
                + jnp.dot(lo, sel, preferred_element_type=F32))

    def twice(x):
        return jnp.concatenate([x, x], axis=1)

    rep = [replicate(cl) for cl in col]

    def gate(c, h, kind):
        lo = (3 * h + kind) * LANES
        return rep[c][:, lo:lo + LANES]

    mm, w_inter, decay, w_state = {}, {}, {}, {}
    for h in range(nh):
        m = m_ref[h:h + 1, :]
        for c in range(n_chunks):
            mm[c, h] = jnp.maximum(m, gate(c, h, 0))
            mm_last = mm[c, h][L - 1:L, :]
            w_inter[c, h] = jnp.exp(m - mm[c, h])
            decay[c, h] = jnp.exp(m - mm_last)
            w_state[c, h] = jnp.exp(gate(c, h, 1) - mm_last)
            m = gate(c, h, 2)[L - 1:L, :] + mm_last
        m_ref[h:h + 1, :] = m

    s = {}
    for u in units:
        c, h = u
        g_row = col_t[c][COL_G + h:COL_G + h + 1, :]
        w_intra = jnp.exp(jnp.where(causal, g_row - twice(mm[u]), -jnp.inf))
        s[u] = lax.dot_general(q[u], k[u], nt, preferred_element_type=F32) * w_intra
    intra = {u: jnp.dot(s[u].astype(BF16), v_ext[u][:, 0:MLSTM_DV], preferred_element_type=F32)
             for u in units}
    upd = {u: lax.dot_general((k[u].astype(F32) * w_state[u]).astype(BF16), v_ext[u], tn,
                              preferred_element_type=F32) for u in units}

    c_in = {}
    for h in range(nh):
        state = c_ref[h]
        for c in range(n_chunks):
            c_in[c, h] = state.astype(BF16)
            d = decay[c, h]
            state = jnp.concatenate([d, d, d], axis=1) * state + upd[c, h]
        c_ref[h] = state

    def lane_sum(x):
        return jnp.sum(x[:, 0:LANES] + x[:, LANES:2 * LANES], axis=-1, keepdims=True)

    for u in units:
        c, h = u
        inter = jnp.dot(q[u], c_in[u], preferred_element_type=F32)
        num = intra[u] + twice(w_inter[u]) * inter[:, 0:MLSTM_DV]
        nq = lane_sum(s[u]) + w_inter[u][:, 0:1] * inter[:, MLSTM_DV:MLSTM_DV + 1]
        m_t = gate(c, h, 2)[:, 0:1] + mm[u][:, 0:1]
        rden = 1.0 / jnp.maximum(jnp.abs(nq), jnp.exp(-m_t))
        msq = lane_sum(num * num) * (1.0 / MLSTM_DV)
        scale = rden * lax.rsqrt(rden * rden * msq + NORM_EPS)
        hn = num * scale * hg_ref[:, dvs[h]]
        y_ref[rows[c], dvs[h]] = (o_ref[rows[c], dvs[h]] * hn).astype(BF16)


def _mlstm(l, qk, v, o_sig, cols, head_g, sel):
    b, s, _ = qk.shape
    assert s % MLSTM_CHUNK == 0, (s, MLSTM_CHUNK)
    t = _pick_tile(s // MLSTM_CHUNK, MLSTM_STEP_CHUNKS) * MLSTM_CHUNK
    row = lambda i, j: (i, j, 0)
    return pl.pallas_call(
        _mlstm_kernel,
        grid=(b, s // t),
        in_specs=[
            pl.BlockSpec((None, t, QK_W), row),
            pl.BlockSpec((None, t, V_W), row),
            pl.BlockSpec((None, t, V_W), row),
            pl.BlockSpec((None, t, LANES), row),
            pl.BlockSpec((None, 1, V_W), lambda i, j: (l, 0, 0)),
            pl.BlockSpec(sel.shape, lambda i, j: (0, 0)),
        ],
        out_specs=pl.BlockSpec((None, t, V_W), row),
        out_shape=jax.ShapeDtypeStruct((b, s, V_W), BF16),
        scratch_shapes=[pltpu.VMEM((MLSTM_HEADS, MLSTM_DQK, MLSTM_DV + LANES), F32),
                        pltpu.VMEM((SUBLANES, LANES), F32)],
        compiler_params=_params("arbitrary", "arbitrary"),
        name="mlstm",
    )(qk, v, o_sig, cols, head_g, sel)


def _block_diag(band, kv_head):
    lane = lax.broadcasted_iota(jnp.int32, band.shape, 1)
    low = lane < ATTN_HEAD_DIM
    rolled = pltpu.roll(band, ATTN_HEAD_DIM, 1)
    if kv_head == 0:
        top = jnp.where(low, band, 0.0)
        bot = jnp.where(low, 0.0, rolled)
    else:
        top = jnp.where(low, rolled, 0.0)
        bot = jnp.where(low, 0.0, band)
    return jnp.concatenate([top, bot], axis=0).astype(BF16)


def _attn_kernel(q_ref, kc_ref, kp_ref, vc_ref, vp_ref, sink_ref, y_ref):
    W = WINDOW
    n_blocks = q_ref.shape[0] // W
    prev_penalty = jnp.where(pl.program_id(1) == 0, -jnp.inf, 0.0)

    key = lax.broadcasted_iota(jnp.int32, (W, PAIRS * W), 0)
    qry = lax.broadcasted_iota(jnp.int32, (W, PAIRS * W), 1) % W
    use_prev = key > qry

    def softmax_t(prev, cur, sink):
        comb = jnp.where(use_prev, prev, cur)
        m = jnp.maximum(jnp.max(comb, axis=0, keepdims=True), sink)
        p = jnp.exp(comb - m)
        den = jnp.sum(p, axis=0, keepdims=True) + jnp.exp(sink - m)
        p = p * (1.0 / den)
        return jnp.where(use_prev, p, 0.0), jnp.where(use_prev, 0.0, p)

    for blk in range(n_blocks):
        rows = slice(blk * W, (blk + 1) * W)
        if blk == 0:
            k_prev, v_prev = kp_ref[...], vp_ref[...]
        else:
            prev = slice((blk - 1) * W, blk * W)
            k_prev, v_prev = kc_ref[prev, :], vc_ref[prev, :]
        k_band = jnp.concatenate([k_prev, kc_ref[rows, :]], axis=0)
        v_band = jnp.concatenate([v_prev, vc_ref[rows, :]], axis=0)
        for kvh in range(ATTN_KV_HEADS):
            k_bd = _block_diag(k_band, kvh)
            v_bd = _block_diag(v_band, kvh)
            lanes = [slice((kvh * PAIRS + p) * LANES, (kvh * PAIRS + p + 1) * LANES)
                     for p in range(PAIRS)]
            q_all = jnp.concatenate([q_ref[rows, ln] for ln in lanes], axis=0)
            st = lax.dot_general(k_bd, q_all, (((1,), (1,)), ((), ())),
                                 preferred_element_type=F32)
            prev_e, cur_e, prev_o, cur_o = (st[i * W:(i + 1) * W, :] for i in range(4))
            if blk == 0:
                prev_e = prev_e + prev_penalty
                prev_o = prev_o + prev_penalty
            pe_prev, pe_cur = softmax_t(prev_e, cur_e, sink_ref[2 * kvh:2 * kvh + 1, :])
            po_prev, po_cur = softmax_t(prev_o, cur_o, sink_ref[2 * kvh + 1:2 * kvh + 2, :])
            pt = jnp.concatenate([pe_prev, pe_cur, po_prev, po_cur], axis=0).astype(BF16)
            out_t = lax.dot_general(v_bd, pt, (((0,), (0,)), ((), ())),
                                    preferred_element_type=F32)
            for p in range(PAIRS):
                y_ref[rows, lanes[p]] = out_t[:, p * W:(p + 1) * W].T.astype(BF16)


def _attention(l, qa, ka, va, sink_rows):
    b, s, _ = qa.shape
    tq = _pick_tile(s, TQ_ATTN)
    per = tq // WINDOW
    cur = lambda i, j: (i, j, 0)
    prev = lambda i, j: (i, jnp.maximum(j * per - 1, 0), 0)
    return pl.pallas_call(
        _attn_kernel,
        grid=(b, s // tq),
        in_specs=[
            pl.BlockSpec((None, tq, QA_W), cur),
            pl.BlockSpec((None, tq, KV_W), cur),
            pl.BlockSpec((None, WINDOW, KV_W), prev),
            pl.BlockSpec((None, tq, KV_W), cur),
            pl.BlockSpec((None, WINDOW, KV_W), prev),
            pl.BlockSpec((None, 2 * ATTN_KV_HEADS, PAIRS * WINDOW), lambda i, j: (l, 0, 0)),
        ],
        out_specs=pl.BlockSpec((None, tq, QA_W), cur),
        out_shape=jax.ShapeDtypeStruct((b, s, QA_W), BF16),
        compiler_params=_params("arbitrary", "arbitrary"),
        name="swa_attention",
    )(qa, ka, ka, va, va, sink_rows)


def _outproj_kernel(ym_ref, ya_ref, x_ref, gate_ref, w_ref, o_ref):
    km = ym_ref.shape[1]
    y = jnp.dot(ym_ref[...], w_ref[0:km, :], preferred_element_type=F32)
    y = y + jnp.dot(ya_ref[...], w_ref[km:, :], preferred_element_type=F32)
    o_ref[...] = x_ref[...] + gate_ref[...] * y


def _outproj(l, ym, ya, x, mod, w_out):
    b, s, d = x.shape
    tm = _pick_tile(s, TM_PROJ)
    row = lambda i, j: (i, j, 0)
    return pl.pallas_call(
        _outproj_kernel,
        grid=(b, s // tm),
        in_specs=[
            pl.BlockSpec((None, tm, ym.shape[2]), row),
            pl.BlockSpec((None, tm, ya.shape[2]), row),
            pl.BlockSpec((None, tm, d), row),
            pl.BlockSpec((None, None, None, 1, d), lambda i, j: (l, i, 2, 0, 0)),
            pl.BlockSpec((None,) + w_out.shape[1:], lambda i, j: (l, 0, 0)),
        ],
        out_specs=pl.BlockSpec((None, tm, d), row),
        out_shape=jax.ShapeDtypeStruct((b, s, d), F32),
        compiler_params=_params("arbitrary", "arbitrary"),
        name="out_proj",
    )(ym, ya, x, mod, w_out)


def _shift_rows(x, prev_tail, k):
    rolled = pltpu.roll(x, k, 0)
    sub = lax.broadcasted_iota(jnp.int32, prev_tail.shape, 0)
    head = jnp.where(sub < k, pltpu.roll(prev_tail, k, 0), rolled[0:SUBLANES, :])
    return jnp.concatenate([head, rolled[SUBLANES:, :]], axis=0)


def _ffn_kernel(x_ref, shift_ref, scale_ref, gate_ref, g_ref, fg_ref, wg_ref, wu_ref, cw_ref, wd_ref,
                o_ref, hb_ref, gs0_ref, gs1_ref, us_ref, halo_ref, tail_ref, *, nf, final_norm):
    tm = x_ref.shape[0]
    halo = SUBLANES
    i = pl.program_id(1)
    j = pl.program_id(2)

    gs_slots = (gs0_ref, gs1_ref)
    tf = wg_ref.shape[1]
    n_up = tf // MXU_WIDTH if tf % MXU_WIDTH == 0 else 1
    upw = tf // n_up

    def up_g_piece(slot, n):
        cols = slice(n * upw, (n + 1) * upw)
        gs_slots[slot][:, cols] = jnp.dot(hb_ref[...], wg_ref[:, cols],
                                          preferred_element_type=F32)

    def up_u():
        us_ref[...] = jnp.dot(hb_ref[...], wu_ref[...], preferred_element_type=F32)

    def fill_halo():
        @pl.when(i == 0)
        def _():
            halo_ref[...] = jnp.zeros(halo_ref.shape, F32)

        @pl.when(i > 0)
        def _():
            halo_ref[...] = tail_ref[j - 1]

    n_rb = FFN_ROW_BLOCKS if tm % (FFN_ROW_BLOCKS * SUBLANES) == 0 else 1
    rb = tm // n_rb

    def gated_down(old, new=None):
        gs_ref = gs_slots[old]
        cw = cw_ref[...] * 0.5
        pending = list(range(n_up)) if new is not None else []
        for r in range(n_rb):
            if pending:
                up_g_piece(new, pending.pop(0))
            rows = slice(r * rb, (r + 1) * rb)
            g = gs_ref[rows, :]
            prev = halo_ref[...] if r == 0 else gs_ref[r * rb - halo:r * rb, :]
            half_c = _shift_rows(g, prev, 2) * cw[0:1, :]
            half_c = half_c + _shift_rows(g, prev, 1) * cw[1:2, :]
            half_c = half_c + g * cw[2:3, :]
            act = (half_c * (1.0 + jnp.tanh(half_c)) * us_ref[rows, :]).astype(BF16)
            o_ref[rows, :] += jnp.dot(act, wd_ref[...], preferred_element_type=F32)
        for n in pending:
            up_g_piece(new, n)
        tail_ref[j - 1] = gs_ref[tm - halo:tm, :]

    @pl.when(j == 0)
    def _():
        hb_ref[...] = _adaln(x_ref[...], g_ref[...], shift_ref[...], scale_ref[...]).astype(BF16)
        o_ref[...] = jnp.zeros(o_ref.shape, F32)
        for n in range(n_up):
            up_g_piece(0, n)
        up_u()

    for par in range(2):
        @pl.when((j > 0) & (j < nf) & (lax.rem(j, 2) == par))
        def _():
            fill_halo()
            gated_down(1 - par, par)
            up_u()

    @pl.when(j == nf)
    def _():
        fill_halo()
        gated_down((nf - 1) % 2)
        res = x_ref[...] + gate_ref[...] * o_ref[...]
        if final_norm:
            ms = jnp.mean(res * res, axis=-1, keepdims=True)
            res = res * lax.rsqrt(ms + NORM_EPS) * fg_ref[...]
        o_ref[...] = res


def _ffn(l, x, mod, g, final_g, w_up, conv_ffn, w_down, final_norm):
    b, s, d = x.shape
    ff = w_down.shape[1]
    tm, tf = _ffn_tiles(s, ff)
    nf = ff // tf
    row = lambda i, t, j: (i, t, 0)
    mod_spec = lambda k: pl.BlockSpec((None, None, None, 1, d), lambda i, t, j: (l, i, k, 0, 0))
    up_tile = lambda j: jnp.minimum(j, nf - 1)
    down_tile = lambda j: jnp.maximum(j - 1, 0)
    return pl.pallas_call(
        functools.partial(_ffn_kernel, nf=nf, final_norm=final_norm),
        grid=(b, s // tm, nf + 1),
        in_specs=[
            pl.BlockSpec((None, tm, d), row),
            mod_spec(3), mod_spec(4), mod_spec(5),
            pl.BlockSpec((None, 1, d), lambda i, t, j: (l, 0, 0)),
            pl.BlockSpec((1, d), lambda i, t, j: (0, 0)),
            pl.BlockSpec((None, d, tf), lambda i, t, j: (l, 0, up_tile(j))),
            pl.BlockSpec((None, d, tf), lambda i, t, j: (l, 0, nf + up_tile(j))),
            pl.BlockSpec((None, FFN_CONV, tf), lambda i, t, j: (l, 0, down_tile(j))),
            pl.BlockSpec((None, tf, d), lambda i, t, j: (l, down_tile(j), 0)),
        ],
        out_specs=pl.BlockSpec((None, tm, d), row),
        out_shape=jax.ShapeDtypeStruct((b, s, d), F32),
        scratch_shapes=[pltpu.VMEM((tm, d), BF16),
                        pltpu.VMEM((tm, tf), F32),
                        pltpu.VMEM((tm, tf), F32),
                        pltpu.VMEM((tm, tf), F32),
                        pltpu.VMEM((SUBLANES, tf), F32),
                        pltpu.VMEM((nf, SUBLANES, tf), F32)],
        compiler_params=_params("arbitrary", "arbitrary", "arbitrary"),
        name="convglu_ffn",
    )(x, mod, mod, mod, g, final_g, w_up, w_up, conv_ffn, w_down)


def _pack_w_in(w_in):
    nh = MLSTM_HEADS
    o_gates = QK_W + 2 * V_W
    o_attn = o_gates + 2 * nh
    gates = w_in[:, :, o_gates:o_attn]
    gates = jnp.pad(gates, ((0, 0), (0, 0), (0, LANES - 2 * nh)))
    packed = jnp.concatenate([w_in[:, :, :o_gates], w_in[:, :, o_attn:], gates], axis=-1)
    return packed.astype(BF16)


def _sink_rows(sinks):
    depth = sinks.shape[0]
    t = sinks.reshape(depth, ATTN_KV_HEADS, PAIRS, 2).transpose(0, 1, 3, 2)
    t = jnp.repeat(t[..., None], WINDOW, axis=-1)
    return t.reshape(depth, 2 * ATTN_KV_HEADS, PAIRS * WINDOW)


def kernel(x, c, positions, ada_w, ada_b, norm_mix_g, w_in, b_gates, conv_qk, mlstm_head_g, sinks,
           w_out, norm_ffn_g, w_up, conv_ffn, w_down, final_g):
    b, s, d = x.shape
    depth = ada_w.shape[0]
    nh = MLSTM_HEADS

    c_pad = jnp.pad(c, ((0, 2 * SUBLANES - b), (0, 0)))
    mod = _modulation(c_pad, ada_w, ada_b)[:, :b].reshape(depth, b, 6, 1, d)
    cos, sina, sinb = _rope_tables(positions)

    w_in_p = _pack_w_in(w_in)
    w_out_b = w_out.astype(BF16)
    w_up_b = w_up.astype(BF16)
    w_down_b = w_down.astype(BF16)
    bg = jnp.pad(b_gates, ((0, 0), (0, LANES - 2 * nh))).reshape(depth, 1, LANES)
    norm_mix = norm_mix_g.reshape(depth, 1, d)
    norm_ffn = norm_ffn_g.reshape(depth, 1, d)
    head_g = mlstm_head_g.reshape(depth, 1, V_W)
    sink_rows = _sink_rows(sinks)
    gate_sel = _gate_selector()
    final_g2 = final_g.reshape(1, d)

    for l in range(depth):
        qk, v, o_sig, cols, qa, ka, va = _inproj(
            l, x, mod, norm_mix, w_in_p, conv_qk, bg, cos, sina, sinb)
        ym = _mlstm(l, qk, v, o_sig, cols, head_g, gate_sel)
        ya = _attention(l, qa, ka, va, sink_rows)
        x = _outproj(l, ym, ya, x, mod, w_out_b)
        x = _ffn(l, x, mod, norm_ffn, final_g2, w_up_b, conv_ffn, w_down_b,
                 final_norm=(l == depth - 1))
    return x
```

```python
import functools

import jax
import jax.numpy as jnp
from jax import lax
from jax.experimental import pallas as pl
from jax.experimental.pallas import tpu as pltpu

F32 = jnp.float32
BF16 = jnp.bfloat16

MLSTM_HEADS = 4
MLSTM_DQK = 128
MLSTM_DV = 256
MLSTM_CONV = 4
MLSTM_CHUNK = 256
GATE_SOFTCAP = 15.0
ATTN_HEAD_DIM = 64
ATTN_HEADS = 16
ATTN_KV_HEADS = 2
WINDOW = 128
ROPE_DIM = 16
ROPE_THETA = 500000.0
FFN_CONV = 3
FFN_ROW_BLOCKS = 4
NORM_EPS = 1e-6

LANES = 128
SUBLANES = 8
MXU_WIDTH = 256
VMEM_LIMIT = 61 * 1024 * 1024

TM_PROJ = 512
TM_OUT = 1024
TQ_ATTN = 1024
MLSTM_STEP_CHUNKS = 4
TM_FFN = 1024
TF_FFN = 512
TN_MOD = 1024
TS_ROPE = 1024

QK_W = 2 * MLSTM_HEADS * MLSTM_DQK
V_W = MLSTM_HEADS * MLSTM_DV
QA_W = ATTN_HEADS * ATTN_HEAD_DIM
KV_W = ATTN_KV_HEADS * ATTN_HEAD_DIM
OFF_QK = 0
OFF_V = OFF_QK + QK_W
OFF_O = OFF_V + V_W
OFF_QA = OFF_O + V_W
OFF_KA = OFF_QA + QA_W
OFF_VA = OFF_KA + KV_W
OFF_G = OFF_VA + KV_W
IN_PACKED = OFF_G + LANES

COL_G = 0
COL_B = MLSTM_HEADS
COL_CG = 2 * MLSTM_HEADS
PAIRS = ATTN_HEADS // ATTN_KV_HEADS // 2


def _params(*sem):
    return pltpu.CompilerParams(dimension_semantics=sem, vmem_limit_bytes=VMEM_LIMIT)


def _pick_tile(n, pref):
    t = min(n, pref)
    while n % t:
        t //= 2
    return t


def _ffn_tiles(s, ff):
    return _pick_tile(s, TM_FFN), _pick_tile(ff, TF_FFN)


def _mod_kernel(c_ref, w_ref, b_ref, o_ref):
    ca = jax.nn.silu(c_ref[...]).astype(BF16)
    o_ref[...] = jnp.dot(ca, w_ref[...].astype(BF16), preferred_element_type=F32) + b_ref[...]


def _modulation(c_pad, ada_w, ada_b):
    depth, d, n = ada_w.shape
    rows = c_pad.shape[0]
    tn = _pick_tile(n, TN_MOD)
    return pl.pallas_call(
        _mod_kernel,
        grid=(depth, n // tn),
        in_specs=[
            pl.BlockSpec((rows, d), lambda l, j: (0, 0)),
            pl.BlockSpec((None, d, tn), lambda l, j: (l, 0, j)),
            pl.BlockSpec((None, 1, tn), lambda l, j: (l, 0, j)),
        ],
        out_specs=pl.BlockSpec((None, rows, tn), lambda l, j: (l, 0, j)),
        out_shape=jax.ShapeDtypeStruct((depth, rows, n), F32),
        compiler_params=_params("arbitrary", "arbitrary"),
        name="adaln_mod",
    )(c_pad, ada_w, ada_b.reshape(depth, 1, n))


def _rope_kernel(pos_ref, freq_ref, cos_ref, sina_ref, sinb_ref):
    ang = pos_ref[...].astype(F32) * freq_ref[...]
    d = lax.broadcasted_iota(jnp.int32, ang.shape, 1) % ATTN_HEAD_DIM
    c = jnp.cos(ang)
    s = jnp.sin(ang)
    half = ROPE_DIM // 2
    cos_ref[...] = jnp.where(d < ROPE_DIM, c, 1.0)
    sina_ref[...] = jnp.where(d < half, -s, 0.0)
    sinb_ref[...] = jnp.where((d >= half) & (d < ROPE_DIM), s, 0.0)


def _rope_tables(positions):
    b, s = positions.shape
    ts = _pick_tile(s, TS_ROPE)
    inv_freq = ROPE_THETA ** (-jnp.arange(0, ROPE_DIM, 2, dtype=F32) / ROPE_DIM)
    per_head = jnp.concatenate(
        [inv_freq, inv_freq, jnp.zeros((ATTN_HEAD_DIM - ROPE_DIM,), F32)])
    freq = jnp.tile(per_head, LANES // ATTN_HEAD_DIM).reshape(1, LANES)
    tab = jax.ShapeDtypeStruct((b, s, LANES), F32)
    spec = pl.BlockSpec((None, ts, LANES), lambda i, j: (i, j, 0))
    return pl.pallas_call(
        _rope_kernel,
        grid=(b, s // ts),
        in_specs=[pl.BlockSpec((None, ts, 1), lambda i, j: (i, j, 0)),
                  pl.BlockSpec((1, LANES), lambda i, j: (0, 0))],
        out_specs=[spec, spec, spec],
        out_shape=[tab, tab, tab],
        compiler_params=_params("arbitrary", "arbitrary"),
        name="rope_tables",
    )(positions.reshape(b, s, 1), freq)


def _adaln(x, g, shift, scale):
    ms = jnp.mean(x * x, axis=-1, keepdims=True)
    h = x * lax.rsqrt(ms + NORM_EPS) * g
    return h * (1.0 + scale) + shift


def _rope(t, cos, sina, sinb):
    n = t.shape[-1]
    half = ROPE_DIM // 2
    return t * cos + pltpu.roll(t, n - half, 1) * sina + pltpu.roll(t, half, 1) * sinb


def _chunk_scan_rows(x, op, fill):
    pos = lax.broadcasted_iota(jnp.int32, x.shape, 0) % MLSTM_CHUNK
    step = 1
    while step < MLSTM_CHUNK:
        x = op(x, jnp.where(pos >= step, pltpu.roll(x, step, 0), fill))
        step *= 2
    return x


def _gate_columns(gates):
    nh = MLSTM_HEADS
    lane = lax.broadcasted_iota(jnp.int32, gates.shape, 1)
    b = _chunk_scan_rows(jax.nn.log_sigmoid(gates), jnp.add, 0.0)
    g = gates - pltpu.roll(b, LANES - nh, 1)
    cg = _chunk_scan_rows(g, jnp.maximum, -jnp.inf)
    return jnp.where(lane < COL_B, g, jnp.where(lane < COL_CG, b, pltpu.roll(cg, COL_CG, 1)))


def _inproj_kernel(x_ref, shift_ref, scale_ref, g_ref, w_ref, cw_ref, bg_ref,
                   cos_ref, sina_ref, sinb_ref,
                   qk_ref, v_ref, o_ref, col_ref, qa_ref, ka_ref, va_ref,
                   hb_ref, cbuf_ref):
    tm = x_ref.shape[0]
    halo = SUBLANES

    @pl.when(pl.program_id(1) == 0)
    def _():
        cbuf_ref[...] = jnp.zeros(cbuf_ref.shape, F32)

    hb_ref[...] = _adaln(x_ref[...], g_ref[...], shift_ref[...], scale_ref[...]).astype(BF16)

    def proj(off, width):
        return jnp.dot(hb_ref[...], w_ref[:, off:off + width], preferred_element_type=F32)

    pg = proj(OFF_G, LANES) + bg_ref[...]
    col_ref[...] = _gate_columns(GATE_SOFTCAP * jnp.tanh(pg / GATE_SOFTCAP))

    cw = cw_ref[...]
    n_parts = 4
    part = QK_W // n_parts
    for c in range(n_parts):
        lo = c * part
        pq = proj(OFF_QK + lo, part)
        prev = cbuf_ref[:, lo:lo + part]
        acc = _shift_rows(pq, prev, 3) * cw[0:1, lo:lo + part]
        acc = acc + _shift_rows(pq, prev, 2) * cw[1:2, lo:lo + part]
        acc = acc + _shift_rows(pq, prev, 1) * cw[2:3, lo:lo + part]
        acc = acc + pq * cw[3:4, lo:lo + part]
        cbuf_ref[:, lo:lo + part] = pq[tm - halo:tm, :]
        act = jax.nn.silu(acc)
        if lo < QK_W // 2:
            act = act * (MLSTM_DQK ** -0.5)
        qk_ref[:, lo:lo + part] = act.astype(BF16)
        o_ref[:, lo:lo + part] = jax.nn.sigmoid(proj(OFF_O + lo, part))

    cos = cos_ref[...]
    sina = sina_ref[...]
    sinb = sinb_ref[...]
    for c in range(QA_W // MXU_WIDTH):
        pa = proj(OFF_QA + c * MXU_WIDTH, MXU_WIDTH)
        for k in range(MXU_WIDTH // LANES):
            lo = c * MXU_WIDTH + k * LANES
            r = _rope(pa[:, k * LANES:(k + 1) * LANES], cos, sina, sinb)
            qa_ref[:, lo:lo + LANES] = (r * (ATTN_HEAD_DIM ** -0.5)).astype(BF16)

    pkv = proj(OFF_KA, 2 * KV_W)
    ka_ref[...] = _rope(pkv[:, 0:KV_W], cos, sina, sinb)
    va_ref[...] = pkv[:, KV_W:2 * KV_W]

    for c in range(2):
        lo = c * (V_W // 2)
        v_ref[:, lo:lo + V_W // 2] = proj(OFF_V + lo, V_W // 2).astype(BF16)


def _inproj(l, x, mod, g, w_packed, conv_qk, bg, cos, sina, sinb):
    b, s, d = x.shape
    tm = _pick_tile(s, TM_PROJ)
    row = lambda i, j: (i, j, 0)
    lay = lambda i, j: (l, 0, 0)
    outs = [
        (QK_W, BF16), (V_W, BF16), (V_W, F32), (LANES, F32), (QA_W, BF16), (KV_W, F32), (KV_W, F32),
    ]
    return pl.pallas_call(
        _inproj_kernel,
        grid=(b, s // tm),
        in_specs=[
            pl.BlockSpec((None, tm, d), row),
            pl.BlockSpec((None, None, None, 1, d), lambda i, j: (l, i, 0, 0, 0)),
            pl.BlockSpec((None, None, None, 1, d), lambda i, j: (l, i, 1, 0, 0)),
            pl.BlockSpec((None, 1, d), lay),
            pl.BlockSpec((None, d, IN_PACKED), lay),
            pl.BlockSpec((None, MLSTM_CONV, QK_W), lay),
            pl.BlockSpec((None, 1, LANES), lay),
            pl.BlockSpec((None, tm, LANES), row),
            pl.BlockSpec((None, tm, LANES), row),
            pl.BlockSpec((None, tm, LANES), row),
        ],
        out_specs=[pl.BlockSpec((None, tm, w), row) for w, _ in outs],
        out_shape=[jax.ShapeDtypeStruct((b, s, w), dt) for w, dt in outs],
        scratch_shapes=[pltpu.VMEM((tm, d), BF16), pltpu.VMEM((SUBLANES, QK_W), F32)],
        compiler_params=_params("arbitrary", "arbitrary"),
        name="in_proj",
    )(x, mod, mod, g, w_packed, conv_qk, bg, cos, sina, sinb)


def _gate_selector():
    nh = MLSTM_HEADS
    src = jnp.arange(LANES)[:, None]
    group = jnp.arange(3 * nh * LANES)[None, :] // LANES
    base = jnp.asarray([COL_CG, COL_G, COL_B])[group % 3]
    return (src == base + group // 3).astype(BF16)


def _mlstm_kernel(qk_ref, v_ref, o_ref, col_ref, hg_ref, sel_ref, y_ref, c_ref, m_ref):
    L = MLSTM_CHUNK
    nh = MLSTM_HEADS
    n_chunks = qk_ref.shape[0] // L

    @pl.when(pl.program_id(1) == 0)
    def _():
        c_ref[...] = jnp.zeros(c_ref.shape, F32)
        m_ref[...] = jnp.zeros(m_ref.shape, F32)

    ri = lax.broadcasted_iota(jnp.int32, (L, L), 0)
    ci = lax.broadcasted_iota(jnp.int32, (L, L), 1)
    causal = ci <= ri
    ones_col = (lax.broadcasted_iota(jnp.int32, (L, LANES), 1) == 0).astype(BF16)
    nt = (((1,), (1,)), ((), ()))
    tn = (((0,), (0,)), ((), ()))
    units = [(c, h) for c in range(n_chunks) for h in range(nh)]
    rows = [slice(c * L, (c + 1) * L) for c in range(n_chunks)]
    dvs = [slice(h * MLSTM_DV, (h + 1) * MLSTM_DV) for h in range(nh)]

    col = [col_ref[r, :] for r in rows]
    col_t = [cl.T for cl in col]
    q = {(c, h): qk_ref[rows[c], h * MLSTM_DQK:(h + 1) * MLSTM_DQK] for c, h in units}
    k = {(c, h): qk_ref[rows[c], (nh + h) * MLSTM_DQK:(nh + h + 1) * MLSTM_DQK] for c, h in units}
    v_ext = {(c, h): jnp.concatenate([v_ref[rows[c], dvs[h]], ones_col], axis=1) for c, h in units}

    def replicate(x):
        hi = x.astype(BF16)
        rem = x - hi.astype(F32)
        mid = rem.astype(BF16)
        lo = (rem - mid.astype(F32)).astype(BF16)
        sel = sel_ref[...]
        return ((jnp.dot(hi, sel, preferred_element_type=F32)
                 + jnp.dot(mid, sel, preferred_element_type=F32))
                + jnp.dot(lo, sel, preferred_element_type=F32))

    def twice(x):
        return jnp.concatenate([x, x], axis=1)

    rep = [replicate(cl) for cl in col]

    def gate(c, h, kind):
        lo = (3 * h + kind) * LANES
        return rep[c][:, lo:lo + LANES]

    mm, w_inter, decay, w_state = {}, {}, {}, {}
    for h in range(nh):
        m = m_ref[h:h + 1, :]
        for c in range(n_chunks):
            mm[c, h] = jnp.maximum(m, gate(c, h, 0))
            mm_last = mm[c, h][L - 1:L, :]
            w_inter[c, h] = jnp.exp(m - mm[c, h])
            decay[c, h] = jnp.exp(m - mm_last)
            w_state[c, h] = jnp.exp(gate(c, h, 1) - mm_last)
            m = gate(c, h, 2)[L - 1:L, :] + mm_last
        m_ref[h:h + 1, :] = m

    s = {}
    for u in units:
        c, h = u
        g_row = col_t[c][COL_G + h:COL_G + h + 1, :]
        w_intra = jnp.exp(jnp.where(causal, g_row - twice(mm[u]), -jnp.inf))
        s[u] = lax.dot_general(q[u], k[u], nt, preferred_element_type=F32) * w_intra
    intra = {u: jnp.dot(s[u].astype(BF16), v_ext[u][:, 0:MLSTM_DV], preferred_element_type=F32)
             for u in units}
    upd = {u: lax.dot_general((k[u].astype(F32) * w_state[u]).astype(BF16), v_ext[u], tn,
                              preferred_element_type=F32) for u in units}

    c_in = {}
    for h in range(nh):
        state = c_ref[h]
        for c in range(n_chunks):
            c_in[c, h] = state.astype(BF16)
            d = decay[c, h]
            state = jnp.concatenate([d, d, d], axis=1) * state + upd[c, h]
        c_ref[h] = state

    def lane_sum(x):
        return jnp.sum(x[:, 0:LANES] + x[:, LANES:2 * LANES], axis=-1, keepdims=True)

    for u in units:
        c, h = u
        inter = jnp.dot(q[u], c_in[u], preferred_element_type=F32)
        num = intra[u] + twice(w_inter[u]) * inter[:, 0:MLSTM_DV]
        nq = lane_sum(s[u]) + w_inter[u][:, 0:1] * inter[:, MLSTM_DV:MLSTM_DV + 1]
        m_t = gate(c, h, 2)[:, 0:1] + mm[u][:, 0:1]
        rden = 1.0 / jnp.maximum(jnp.abs(nq), jnp.exp(-m_t))
        msq = lane_sum(num * num) * (1.0 / MLSTM_DV)
        scale = rden * lax.rsqrt(rden * rden * msq + NORM_EPS)
        hn = num * scale * hg_ref[:, dvs[h]]
        y_ref[rows[c], dvs[h]] = (o_ref[rows[c], dvs[h]] * hn).astype(BF16)


def _mlstm(l, qk, v, o_sig, cols, head_g, sel):
    b, s, _ = qk.shape
    assert s % MLSTM_CHUNK == 0, (s, MLSTM_CHUNK)
    t = _pick_tile(s // MLSTM_CHUNK, MLSTM_STEP_CHUNKS) * MLSTM_CHUNK
    row = lambda i, j: (i, j, 0)
    return pl.pallas_call(
        _mlstm_kernel,
        grid=(b, s // t),
        in_specs=[
            pl.BlockSpec((None, t, QK_W), row),
            pl.BlockSpec((None, t, V_W), row),
            pl.BlockSpec((None, t, V_W), row),
            pl.BlockSpec((None, t, LANES), row),
            pl.BlockSpec((None, 1, V_W), lambda i, j: (l, 0, 0)),
            pl.BlockSpec(sel.shape, lambda i, j: (0, 0)),
        ],
        out_specs=pl.BlockSpec((None, t, V_W), row),
        out_shape=jax.ShapeDtypeStruct((b, s, V_W), BF16),
        scratch_shapes=[pltpu.VMEM((MLSTM_HEADS, MLSTM_DQK, MLSTM_DV + LANES), F32),
                        pltpu.VMEM((SUBLANES, LANES), F32)],
        compiler_params=_params("arbitrary", "arbitrary"),
        name="mlstm",
    )(qk, v, o_sig, cols, head_g, sel)


def _block_diag(band, kv_head):
    lane = lax.broadcasted_iota(jnp.int32, band.shape, 1)
    low = lane < ATTN_HEAD_DIM
    rolled = pltpu.roll(band, ATTN_HEAD_DIM, 1)
    if kv_head == 0:
        top = jnp.where(low, band, 0.0)
        bot = jnp.where(low, 0.0, rolled)
    else:
        top = jnp.where(low, rolled, 0.0)
        bot = jnp.where(low, 0.0, band)
    return jnp.concatenate([top, bot], axis=0).astype(BF16)


def _attn_kernel(q_ref, kc_ref, kp_ref, vc_ref, vp_ref, sink_ref, y_ref):
    W = WINDOW
    n_blocks = q_ref.shape[0] // W
    prev_penalty = jnp.where(pl.program_id(1) == 0, -jnp.inf, 0.0)

    key = lax.broadcasted_iota(jnp.int32, (W, PAIRS * W), 0)
    qry = lax.broadcasted_iota(jnp.int32, (W, PAIRS * W), 1) % W
    use_prev = key > qry

    def softmax_t(prev, cur, sink):
        comb = jnp.where(use_prev, prev, cur)
        m = jnp.maximum(jnp.max(comb, axis=0, keepdims=True), sink)
        p = jnp.exp(comb - m)
        den = jnp.sum(p, axis=0, keepdims=True) + jnp.exp(sink - m)
        p = p * (1.0 / den)
        return jnp.where(use_prev, p, 0.0), jnp.where(use_prev, 0.0, p)

    for blk in range(n_blocks):
        rows = slice(blk * W, (blk + 1) * W)
        if blk == 0:
            k_prev, v_prev = kp_ref[...], vp_ref[...]
        else:
            prev = slice((blk - 1) * W, blk * W)
            k_prev, v_prev = kc_ref[prev, :], vc_ref[prev, :]
        k_band = jnp.concatenate([k_prev, kc_ref[rows, :]], axis=0)
        v_band = jnp.concatenate([v_prev, vc_ref[rows, :]], axis=0)
        for kvh in range(ATTN_KV_HEADS):
            k_bd = _block_diag(k_band, kvh)
            v_bd = _block_diag(v_band, kvh)
            lanes = [slice((kvh * PAIRS + p) * LANES, (kvh * PAIRS + p + 1) * LANES)
                     for p in range(PAIRS)]
            q_all = jnp.concatenate([q_ref[rows, ln] for ln in lanes], axis=0)
            st = lax.dot_general(k_bd, q_all, (((1,), (1,)), ((), ())),
                                 preferred_element_type=F32)
            prev_e, cur_e, prev_o, cur_o = (st[i * W:(i + 1) * W, :] for i in range(4))
            if blk == 0:
                prev_e = prev_e + prev_penalty
                prev_o = prev_o + prev_penalty
            pe_prev, pe_cur = softmax_t(prev_e, cur_e, sink_ref[2 * kvh:2 * kvh + 1, :])
            po_prev, po_cur = softmax_t(prev_o, cur_o, sink_ref[2 * kvh + 1:2 * kvh + 2, :])
            pt = jnp.concatenate([pe_prev, pe_cur, po_prev, po_cur], axis=0).astype(BF16)
            out_t = lax.dot_general(v_bd, pt, (((0,), (0,)), ((), ())),
                                    preferred_element_type=F32)
            for p in range(PAIRS):
                y_ref[rows, lanes[p]] = out_t[:, p * W:(p + 1) * W].T.astype(BF16)


def _attention(l, qa, ka, va, sink_rows):
    b, s, _ = qa.shape
    tq = _pick_tile(s, TQ_ATTN)
    per = tq // WINDOW
    cur = lambda i, j: (i, j, 0)
    prev = lambda i, j: (i, jnp.maximum(j * per - 1, 0), 0)
    return pl.pallas_call(
        _attn_kernel,
        grid=(b, s // tq),
        in_specs=[
            pl.BlockSpec((None, tq, QA_W), cur),
            pl.BlockSpec((None, tq, KV_W), cur),
            pl.BlockSpec((None, WINDOW, KV_W), prev),
            pl.BlockSpec((None, tq, KV_W), cur),
            pl.BlockSpec((None, WINDOW, KV_W), prev),
            pl.BlockSpec((None, 2 * ATTN_KV_HEADS, PAIRS * WINDOW), lambda i, j: (l, 0, 0)),
        ],
        out_specs=pl.BlockSpec((None, tq, QA_W), cur),
        out_shape=jax.ShapeDtypeStruct((b, s, QA_W), BF16),
        compiler_params=_params("arbitrary", "arbitrary"),
        name="swa_attention",
    )(qa, ka, ka, va, va, sink_rows)


def _outproj_kernel(ym_ref, ya_ref, x_ref, gate_ref, w_ref, o_ref):
    km = ym_ref.shape[1]
    y = jnp.dot(ym_ref[...], w_ref[0:km, :], preferred_element_type=F32)
    y = y + jnp.dot(ya_ref[...], w_ref[km:, :], preferred_element_type=F32)
    o_ref[...] = x_ref[...] + gate_ref[...] * y


def _outproj(l, ym, ya, x, mod, w_out):
    b, s, d = x.shape
    tm = _pick_tile(s, TM_OUT)
    row = lambda i, j: (i, j, 0)
    return pl.pallas_call(
        _outproj_kernel,
        grid=(b, s // tm),
        in_specs=[
            pl.BlockSpec((None, tm, ym.shape[2]), row),
            pl.BlockSpec((None, tm, ya.shape[2]), row),
            pl.BlockSpec((None, tm, d), row),
            pl.BlockSpec((None, None, None, 1, d), lambda i, j: (l, i, 2, 0, 0)),
            pl.BlockSpec((None,) + w_out.shape[1:], lambda i, j: (l, 0, 0)),
        ],
        out_specs=pl.BlockSpec((None, tm, d), row),
        out_shape=jax.ShapeDtypeStruct((b, s, d), F32),
        compiler_params=_params("arbitrary", "arbitrary"),
        name="out_proj",
    )(ym, ya, x, mod, w_out)


def _shift_rows(x, prev_tail, k):
    rolled = pltpu.roll(x, k, 0)
    sub = lax.broadcasted_iota(jnp.int32, prev_tail.shape, 0)
    head = jnp.where(sub < k, pltpu.roll(prev_tail, k, 0), rolled[0:SUBLANES, :])
    return jnp.concatenate([head, rolled[SUBLANES:, :]], axis=0)


def _ffn_kernel(x_ref, shift_ref, scale_ref, gate_ref, g_ref, fg_ref, wg_ref, wu_ref, cw_ref, wd_ref,
                o_ref, hb_ref, gs0_ref, gs1_ref, us_ref, halo_ref, tail_ref, *, nf, final_norm):
    tm = x_ref.shape[0]
    halo = SUBLANES
    i = pl.program_id(1)
    j = pl.program_id(2)

    gs_slots = (gs0_ref, gs1_ref)
    tf = wg_ref.shape[1]
    n_up = tf // MXU_WIDTH if tf % MXU_WIDTH == 0 else 1
    upw = tf // n_up

    def up_g_piece(slot, n):
        cols = slice(n * upw, (n + 1) * upw)
        gs_slots[slot][:, cols] = jnp.dot(hb_ref[...], wg_ref[:, cols],
                                          preferred_element_type=F32)

    def up_u():
        us_ref[...] = jnp.dot(hb_ref[...], wu_ref[...], preferred_element_type=F32)

    def fill_halo():
        @pl.when(i == 0)
        def _():
            halo_ref[...] = jnp.zeros(halo_ref.shape, F32)

        @pl.when(i > 0)
        def _():
            halo_ref[...] = tail_ref[j - 1]

    n_rb = FFN_ROW_BLOCKS if tm % (FFN_ROW_BLOCKS * SUBLANES) == 0 else 1
    rb = tm // n_rb

    def gated_down(old, new=None):
        gs_ref = gs_slots[old]
        cw = cw_ref[...] * 0.5
        pending = list(range(n_up)) if new is not None else []
        for r in range(n_rb):
            if pending:
                up_g_piece(new, pending.pop(0))
            rows = slice(r * rb, (r + 1) * rb)
            g = gs_ref[rows, :]
            prev = halo_ref[...] if r == 0 else gs_ref[r * rb - halo:r * rb, :]
            half_c = _shift_rows(g, prev, 2) * cw[0:1, :]
            half_c = half_c + _shift_rows(g, prev, 1) * cw[1:2, :]
            half_c = half_c + g * cw[2:3, :]
            act = (half_c * (1.0 + jnp.tanh(half_c)) * us_ref[rows, :]).astype(BF16)
            o_ref[rows, :] += jnp.dot(act, wd_ref[...], preferred_element_type=F32)
        for n in pending:
            up_g_piece(new, n)
        tail_ref[j - 1] = gs_ref[tm - halo:tm, :]

    @pl.when(j == 0)
    def _():
        hb_ref[...] = _adaln(x_ref[...], g_ref[...], shift_ref[...], scale_ref[...]).astype(BF16)
        o_ref[...] = jnp.zeros(o_ref.shape, F32)
        for n in range(n_up):
            up_g_piece(0, n)
        up_u()

    for par in range(2):
        @pl.when((j > 0) & (j < nf) & (lax.rem(j, 2) == par))
        def _():
            fill_halo()
            gated_down(1 - par, par)
            up_u()

    @pl.when(j == nf)
    def _():
        fill_halo()
        gated_down((nf - 1) % 2)
        res = x_ref[...] + gate_ref[...] * o_ref[...]
        if final_norm:
            ms = jnp.mean(res * res, axis=-1, keepdims=True)
            res = res * lax.rsqrt(ms + NORM_EPS) * fg_ref[...]
        o_ref[...] = res


def _ffn(l, x, mod, g, final_g, w_up, conv_ffn, w_down, final_norm):
    b, s, d = x.shape
    ff = w_down.shape[1]
    tm, tf = _ffn_tiles(s, ff)
    nf = ff // tf
    row = lambda i, t, j: (i, t, 0)
    mod_spec = lambda k: pl.BlockSpec((None, None, None, 1, d), lambda i, t, j: (l, i, k, 0, 0))
    up_tile = lambda j: jnp.minimum(j, nf - 1)
    down_tile = lambda j: jnp.maximum(j - 1, 0)
    return pl.pallas_call(
        functools.partial(_ffn_kernel, nf=nf, final_norm=final_norm),
        grid=(b, s // tm, nf + 1),
        in_specs=[
            pl.BlockSpec((None, tm, d), row),
            mod_spec(3), mod_spec(4), mod_spec(5),
            pl.BlockSpec((None, 1, d), lambda i, t, j: (l, 0, 0)),
            pl.BlockSpec((1, d), lambda i, t, j: (0, 0)),
            pl.BlockSpec((None, d, tf), lambda i, t, j: (l, 0, up_tile(j))),
            pl.BlockSpec((None, d, tf), lambda i, t, j: (l, 0, nf + up_tile(j))),
            pl.BlockSpec((None, FFN_CONV, tf), lambda i, t, j: (l, 0, down_tile(j))),
            pl.BlockSpec((None, tf, d), lambda i, t, j: (l, down_tile(j), 0)),
        ],
        out_specs=pl.BlockSpec((None, tm, d), row),
        out_shape=jax.ShapeDtypeStruct((b, s, d), F32),
        scratch_shapes=[pltpu.VMEM((tm, d), BF16),
                        pltpu.VMEM((tm, tf), F32),
                        pltpu.VMEM((tm, tf), F32),
                        pltpu.VMEM((tm, tf), F32),
                        pltpu.VMEM((SUBLANES, tf), F32),
                        pltpu.VMEM((nf, SUBLANES, tf), F32)],
        compiler_params=_params("arbitrary", "arbitrary", "arbitrary"),
        name="convglu_ffn",
    )(x, mod, mod, mod, g, final_g, w_up, w_up, conv_ffn, w_down)


def _pack_w_in(w_in):
    nh = MLSTM_HEADS
    o_gates = QK_W + 2 * V_W
    o_attn = o_gates + 2 * nh
    w_in = w_in.astype(BF16)
    gates = w_in[:, :, o_gates:o_attn]
    gates = jnp.pad(gates, ((0, 0), (0, 0), (0, LANES - 2 * nh)))
    return jnp.concatenate([w_in[:, :, :o_gates], w_in[:, :, o_attn:], gates], axis=-1)


def _sink_rows(sinks):
    depth = sinks.shape[0]
    t = sinks.reshape(depth, ATTN_KV_HEADS, PAIRS, 2).transpose(0, 1, 3, 2)
    t = jnp.repeat(t[..., None], WINDOW, axis=-1)
    return t.reshape(depth, 2 * ATTN_KV_HEADS, PAIRS * WINDOW)


def kernel(x, c, positions, ada_w, ada_b, norm_mix_g, w_in, b_gates, conv_qk, mlstm_head_g, sinks,
           w_out, norm_ffn_g, w_up, conv_ffn, w_down, final_g):
    b, s, d = x.shape
    depth = ada_w.shape[0]
    nh = MLSTM_HEADS

    c_pad = jnp.pad(c, ((0, 2 * SUBLANES - b), (0, 0)))
    mod = _modulation(c_pad, ada_w, ada_b)[:, :b].reshape(depth, b, 6, 1, d)
    cos, sina, sinb = _rope_tables(positions)

    w_in_p = _pack_w_in(w_in)
    w_out_b = w_out.astype(BF16)
    w_up_b = w_up.astype(BF16)
    w_down_b = w_down.astype(BF16)
    bg = jnp.pad(b_gates, ((0, 0), (0, LANES - 2 * nh))).reshape(depth, 1, LANES)
    norm_mix = norm_mix_g.reshape(depth, 1, d)
    norm_ffn = norm_ffn_g.reshape(depth, 1, d)
    head_g = mlstm_head_g.reshape(depth, 1, V_W)
    sink_rows = _sink_rows(sinks)
    gate_sel = _gate_selector()
    final_g2 = final_g.reshape(1, d)

    for l in range(depth):
        qk, v, o_sig, cols, qa, ka, va = _inproj(
            l, x, mod, norm_mix, w_in_p, conv_qk, bg, cos, sina, sinb)
        ym = _mlstm(l, qk, v, o_sig, cols, head_g, gate_sel)
        ya = _attention(l, qa, ka, va, sink_rows)
        x = _outproj(l, ym, ya, x, mod, w_out_b)
        x = _ffn(l, x, mod, norm_ffn, final_g2, w_up_b, conv_ffn, w_down_b,
                 final_norm=(l == depth - 1))
    return x
```

```python
import functools

import jax
import jax.numpy as jnp
from jax import lax
from jax.experimental import pallas as pl
from jax.experimental.pallas import tpu as pltpu

F32 = jnp.float32
BF16 = jnp.bfloat16

MLSTM_HEADS = 4
MLSTM_DQK = 128
MLSTM_DV = 256
MLSTM_CONV = 4
MLSTM_CHUNK = 256
GATE_SOFTCAP = 15.0
ATTN_HEAD_DIM = 64
ATTN_HEADS = 16
ATTN_KV_HEADS = 2
WINDOW = 128
ROPE_DIM = 16
ROPE_THETA = 500000.0
FFN_CONV = 3
FFN_ROW_BLOCKS = 4
NORM_EPS = 1e-6

LANES = 128
SUBLANES = 8
MXU_WIDTH = 256
VMEM_LIMIT = 61 * 1024 * 1024

TM_PROJ = 512
TM_OUT = 1024
TQ_ATTN = 1024
ATTN_GROUP_BLOCKS = 4
MLSTM_STEP_CHUNKS = 4
TM_FFN = 1024
TF_FFN = 512
TN_MOD = 1024
TS_ROPE = 1024

QK_W = 2 * MLSTM_HEADS * MLSTM_DQK
V_W = MLSTM_HEADS * MLSTM_DV
QA_W = ATTN_HEADS * ATTN_HEAD_DIM
KV_W = ATTN_KV_HEADS * ATTN_HEAD_DIM
OFF_QK = 0
OFF_V = OFF_QK + QK_W
OFF_O = OFF_V + V_W
OFF_QA = OFF_O + V_W
OFF_KA = OFF_QA + QA_W
OFF_VA = OFF_KA + KV_W
OFF_G = OFF_VA + KV_W
IN_PACKED = OFF_G + LANES

COL_G = 0
COL_B = MLSTM_HEADS
COL_CG = 2 * MLSTM_HEADS
PAIRS = ATTN_HEADS // ATTN_KV_HEADS // 2


def _params(*sem):
    return pltpu.CompilerParams(dimension_semantics=sem, vmem_limit_bytes=VMEM_LIMIT)


def _pick_tile(n, pref):
    t = min(n, pref)
    while n % t:
        t //= 2
    return t


def _ffn_tiles(s, ff):
    return _pick_tile(s, TM_FFN), _pick_tile(ff, TF_FFN)


def _mod_kernel(c_ref, w_ref, b_ref, o_ref):
    ca = jax.nn.silu(c_ref[...]).astype(BF16)
    o_ref[...] = jnp.dot(ca, w_ref[...].astype(BF16), preferred_element_type=F32) + b_ref[...]


def _modulation(c_pad, ada_w, ada_b):
    depth, d, n = ada_w.shape
    rows = c_pad.shape[0]
    tn = _pick_tile(n, TN_MOD)
    return pl.pallas_call(
        _mod_kernel,
        grid=(depth, n // tn),
        in_specs=[
            pl.BlockSpec((rows, d), lambda l, j: (0, 0)),
            pl.BlockSpec((None, d, tn), lambda l, j: (l, 0, j)),
            pl.BlockSpec((None, 1, tn), lambda l, j: (l, 0, j)),
        ],
        out_specs=pl.BlockSpec((None, rows, tn), lambda l, j: (l, 0, j)),
        out_shape=jax.ShapeDtypeStruct((depth, rows, n), F32),
        compiler_params=_params("arbitrary", "arbitrary"),
        name="adaln_mod",
    )(c_pad, ada_w, ada_b.reshape(depth, 1, n))


def _rope_kernel(pos_ref, freq_ref, cos_ref, sina_ref, sinb_ref):
    ang = pos_ref[...].astype(F32) * freq_ref[...]
    d = lax.broadcasted_iota(jnp.int32, ang.shape, 1) % ATTN_HEAD_DIM
    c = jnp.cos(ang)
    s = jnp.sin(ang)
    half = ROPE_DIM // 2
    cos_ref[...] = jnp.where(d < ROPE_DIM, c, 1.0)
    sina_ref[...] = jnp.where(d < half, -s, 0.0)
    sinb_ref[...] = jnp.where((d >= half) & (d < ROPE_DIM), s, 0.0)


def _rope_tables(positions):
    b, s = positions.shape
    ts = _pick_tile(s, TS_ROPE)
    inv_freq = ROPE_THETA ** (-jnp.arange(0, ROPE_DIM, 2, dtype=F32) / ROPE_DIM)
    per_head = jnp.concatenate(
        [inv_freq, inv_freq, jnp.zeros((ATTN_HEAD_DIM - ROPE_DIM,), F32)])
    freq = jnp.tile(per_head, LANES // ATTN_HEAD_DIM).reshape(1, LANES)
    tab = jax.ShapeDtypeStruct((b, s, LANES), F32)
    spec = pl.BlockSpec((None, ts, LANES), lambda i, j: (i, j, 0))
    return pl.pallas_call(
        _rope_kernel,
        grid=(b, s // ts),
        in_specs=[pl.BlockSpec((None, ts, 1), lambda i, j: (i, j, 0)),
                  pl.BlockSpec((1, LANES), lambda i, j: (0, 0))],
        out_specs=[spec, spec, spec],
        out_shape=[tab, tab, tab],
        compiler_params=_params("arbitrary", "arbitrary"),
        name="rope_tables",
    )(positions.reshape(b, s, 1), freq)


def _adaln(x, g, shift, scale):
    ms = jnp.mean(x * x, axis=-1, keepdims=True)
    h = x * lax.rsqrt(ms + NORM_EPS) * g
    return h * (1.0 + scale) + shift


def _rope(t, cos, sina, sinb):
    n = t.shape[-1]
    half = ROPE_DIM // 2
    return t * cos + pltpu.roll(t, n - half, 1) * sina + pltpu.roll(t, half, 1) * sinb


def _chunk_scan_rows(x, op, fill):
    pos = lax.broadcasted_iota(jnp.int32, x.shape, 0) % MLSTM_CHUNK
    step = 1
    while step < MLSTM_CHUNK:
        x = op(x, jnp.where(pos >= step, pltpu.roll(x, step, 0), fill))
        step *= 2
    return x


def _gate_columns(gates):
    nh = MLSTM_HEADS
    lane = lax.broadcasted_iota(jnp.int32, gates.shape, 1)
    b = _chunk_scan_rows(jax.nn.log_sigmoid(gates), jnp.add, 0.0)
    g = gates - pltpu.roll(b, LANES - nh, 1)
    cg = _chunk_scan_rows(g, jnp.maximum, -jnp.inf)
    return jnp.where(lane < COL_B, g, jnp.where(lane < COL_CG, b, pltpu.roll(cg, COL_CG, 1)))


def _inproj_kernel(x_ref, shift_ref, scale_ref, g_ref, w_ref, cw_ref, bg_ref,
                   cos_ref, sina_ref, sinb_ref,
                   qk_ref, v_ref, o_ref, col_ref, qa_ref, ka_ref, va_ref,
                   hb_ref, cbuf_ref):
    tm = x_ref.shape[0]
    halo = SUBLANES

    @pl.when(pl.program_id(1) == 0)
    def _():
        cbuf_ref[...] = jnp.zeros(cbuf_ref.shape, F32)

    hb_ref[...] = _adaln(x_ref[...], g_ref[...], shift_ref[...], scale_ref[...]).astype(BF16)

    def proj(off, width):
        return jnp.dot(hb_ref[...], w_ref[:, off:off + width], preferred_element_type=F32)

    pg = proj(OFF_G, LANES) + bg_ref[...]
    col_ref[...] = _gate_columns(GATE_SOFTCAP * jnp.tanh(pg / GATE_SOFTCAP))

    cw = cw_ref[...]
    n_parts = 4
    part = QK_W // n_parts
    for c in range(n_parts):
        lo = c * part
        pq = proj(OFF_QK + lo, part)
        prev = cbuf_ref[:, lo:lo + part]
        acc = _shift_rows(pq, prev, 3) * cw[0:1, lo:lo + part]
        acc = acc + _shift_rows(pq, prev, 2) * cw[1:2, lo:lo + part]
        acc = acc + _shift_rows(pq, prev, 1) * cw[2:3, lo:lo + part]
        acc = acc + pq * cw[3:4, lo:lo + part]
        cbuf_ref[:, lo:lo + part] = pq[tm - halo:tm, :]
        act = jax.nn.silu(acc)
        if lo < QK_W // 2:
            act = act * (MLSTM_DQK ** -0.5)
        qk_ref[:, lo:lo + part] = act.astype(BF16)
        o_ref[:, lo:lo + part] = jax.nn.sigmoid(proj(OFF_O + lo, part))

    cos = cos_ref[...]
    sina = sina_ref[...]
    sinb = sinb_ref[...]
    for c in range(QA_W // MXU_WIDTH):
        pa = proj(OFF_QA + c * MXU_WIDTH, MXU_WIDTH)
        for k in range(MXU_WIDTH // LANES):
            lo = c * MXU_WIDTH + k * LANES
            r = _rope(pa[:, k * LANES:(k + 1) * LANES], cos, sina, sinb)
            qa_ref[:, lo:lo + LANES] = (r * (ATTN_HEAD_DIM ** -0.5)).astype(BF16)

    pkv = proj(OFF_KA, 2 * KV_W)
    ka_ref[...] = _rope(pkv[:, 0:KV_W], cos, sina, sinb)
    va_ref[...] = pkv[:, KV_W:2 * KV_W]

    for c in range(2):
        lo = c * (V_W // 2)
        v_ref[:, lo:lo + V_W // 2] = proj(OFF_V + lo, V_W // 2).astype(BF16)


def _inproj(l, x, mod, g, w_packed, conv_qk, bg, cos, sina, sinb):
    b, s, d = x.shape
    tm = _pick_tile(s, TM_PROJ)
    row = lambda i, j: (i, j, 0)
    lay = lambda i, j: (l, 0, 0)
    outs = [
        (QK_W, BF16), (V_W, BF16), (V_W, F32), (LANES, F32), (QA_W, BF16), (KV_W, F32), (KV_W, F32),
    ]
    return pl.pallas_call(
        _inproj_kernel,
        grid=(b, s // tm),
        in_specs=[
            pl.BlockSpec((None, tm, d), row),
            pl.BlockSpec((None, None, None, 1, d), lambda i, j: (l, i, 0, 0, 0)),
            pl.BlockSpec((None, None, None, 1, d), lambda i, j: (l, i, 1, 0, 0)),
            pl.BlockSpec((None, 1, d), lay),
            pl.BlockSpec((None, d, IN_PACKED), lay),
            pl.BlockSpec((None, MLSTM_CONV, QK_W), lay),
            pl.BlockSpec((None, 1, LANES), lay),
            pl.BlockSpec((None, tm, LANES), row),
            pl.BlockSpec((None, tm, LANES), row),
            pl.BlockSpec((None, tm, LANES), row),
        ],
        out_specs=[pl.BlockSpec((None, tm, w), row) for w, _ in outs],
        out_shape=[jax.ShapeDtypeStruct((b, s, w), dt) for w, dt in outs],
        scratch_shapes=[pltpu.VMEM((tm, d), BF16), pltpu.VMEM((SUBLANES, QK_W), F32)],
        compiler_params=_params("arbitrary", "arbitrary"),
        name="in_proj",
    )(x, mod, mod, g, w_packed, conv_qk, bg, cos, sina, sinb)


def _gate_selector():
    nh = MLSTM_HEADS
    src = jnp.arange(LANES)[:, None]
    group = jnp.arange(3 * nh * LANES)[None, :] // LANES
    base = jnp.asarray([COL_CG, COL_G, COL_B])[group % 3]
    return (src == base + group // 3).astype(BF16)


def _mlstm_kernel(qk_ref, v_ref, o_ref, col_ref, hg_ref, sel_ref, y_ref, c_ref, m_ref):
    L = MLSTM_CHUNK
    nh = MLSTM_HEADS
    n_chunks = qk_ref.shape[0] // L

    @pl.when(pl.program_id(1) == 0)
    def _():
        c_ref[...] = jnp.zeros(c_ref.shape, F32)
        m_ref[...] = jnp.zeros(m_ref.shape, F32)

    ri = lax.broadcasted_iota(jnp.int32, (L, L), 0)
    ci = lax.broadcasted_iota(jnp.int32, (L, L), 1)
    causal = ci <= ri
    ones_col = (lax.broadcasted_iota(jnp.int32, (L, LANES), 1) == 0).astype(BF16)
    nt = (((1,), (1,)), ((), ()))
    tn = (((0,), (0,)), ((), ()))
    units = [(c, h) for c in range(n_chunks) for h in range(nh)]
    rows = [slice(c * L, (c + 1) * L) for c in range(n_chunks)]
    dvs = [slice(h * MLSTM_DV, (h + 1) * MLSTM_DV) for h in range(nh)]

    col = [col_ref[r, :] for r in rows]
    col_t = [cl.T for cl in col]
    q = {(c, h): qk_ref[rows[c], h * MLSTM_DQK:(h + 1) * MLSTM_DQK] for c, h in units}
    k = {(c, h): qk_ref[rows[c], (nh + h) * MLSTM_DQK:(nh + h + 1) * MLSTM_DQK] for c, h in units}
    v_ext = {(c, h): jnp.concatenate([v_ref[rows[c], dvs[h]], ones_col], axis=1) for c, h in units}

    def replicate(x):
        hi = x.astype(BF16)
        rem = x - hi.astype(F32)
        mid = rem.astype(BF16)
        lo = (rem - mid.astype(F32)).astype(BF16)
        sel = sel_ref[...]
        return ((jnp.dot(hi, sel, preferred_element_type=F32)
                 + jnp.dot(mid, sel, preferred_element_type=F32))
                + jnp.dot(lo, sel, preferred_element_type=F32))

    def twice(x):
        return jnp.concatenate([x, x], axis=1)

    rep = [replicate(cl) for cl in col]

    def gate(c, h, kind):
        lo = (3 * h + kind) * LANES
        return rep[c][:, lo:lo + LANES]

    mm, w_inter, decay, w_state = {}, {}, {}, {}
    for h in range(nh):
        m = m_ref[h:h + 1, :]
        for c in range(n_chunks):
            mm[c, h] = jnp.maximum(m, gate(c, h, 0))
            mm_last = mm[c, h][L - 1:L, :]
            w_inter[c, h] = jnp.exp(m - mm[c, h])
            decay[c, h] = jnp.exp(m - mm_last)
            w_state[c, h] = jnp.exp(gate(c, h, 1) - mm_last)
            m = gate(c, h, 2)[L - 1:L, :] + mm_last
        m_ref[h:h + 1, :] = m

    s = {}
    for u in units:
        c, h = u
        g_row = col_t[c][COL_G + h:COL_G + h + 1, :]
        w_intra = jnp.exp(jnp.where(causal, g_row - twice(mm[u]), -jnp.inf))
        s[u] = lax.dot_general(q[u], k[u], nt, preferred_element_type=F32) * w_intra
    intra = {u: jnp.dot(s[u].astype(BF16), v_ext[u][:, 0:MLSTM_DV], preferred_element_type=F32)
             for u in units}
    upd = {u: lax.dot_general((k[u].astype(F32) * w_state[u]).astype(BF16), v_ext[u], tn,
                              preferred_element_type=F32) for u in units}

    c_in = {}
    for h in range(nh):
        state = c_ref[h]
        for c in range(n_chunks):
            c_in[c, h] = state.astype(BF16)
            d = decay[c, h]
            state = jnp.concatenate([d, d, d], axis=1) * state + upd[c, h]
        c_ref[h] = state

    def lane_sum(x):
        return jnp.sum(x[:, 0:LANES] + x[:, LANES:2 * LANES], axis=-1, keepdims=True)

    for u in units:
        c, h = u
        inter = jnp.dot(q[u], c_in[u], preferred_element_type=F32)
        num = intra[u] + twice(w_inter[u]) * inter[:, 0:MLSTM_DV]
        nq = lane_sum(s[u]) + w_inter[u][:, 0:1] * inter[:, MLSTM_DV:MLSTM_DV + 1]
        m_t = gate(c, h, 2)[:, 0:1] + mm[u][:, 0:1]
        rden = 1.0 / jnp.maximum(jnp.abs(nq), jnp.exp(-m_t))
        msq = lane_sum(num * num) * (1.0 / MLSTM_DV)
        scale = rden * lax.rsqrt(rden * rden * msq + NORM_EPS)
        hn = num * scale * hg_ref[:, dvs[h]]
        y_ref[rows[c], dvs[h]] = (o_ref[rows[c], dvs[h]] * hn).astype(BF16)


def _mlstm(l, qk, v, o_sig, cols, head_g, sel):
    b, s, _ = qk.shape
    assert s % MLSTM_CHUNK == 0, (s, MLSTM_CHUNK)
    t = _pick_tile(s // MLSTM_CHUNK, MLSTM_STEP_CHUNKS) * MLSTM_CHUNK
    row = lambda i, j: (i, j, 0)
    return pl.pallas_call(
        _mlstm_kernel,
        grid=(b, s // t),
        in_specs=[
            pl.BlockSpec((None, t, QK_W), row),
            pl.BlockSpec((None, t, V_W), row),
            pl.BlockSpec((None, t, V_W), row),
            pl.BlockSpec((None, t, LANES), row),
            pl.BlockSpec((None, 1, V_W), lambda i, j: (l, 0, 0)),
            pl.BlockSpec(sel.shape, lambda i, j: (0, 0)),
        ],
        out_specs=pl.BlockSpec((None, t, V_W), row),
        out_shape=jax.ShapeDtypeStruct((b, s, V_W), BF16),
        scratch_shapes=[pltpu.VMEM((MLSTM_HEADS, MLSTM_DQK, MLSTM_DV + LANES), F32),
                        pltpu.VMEM((SUBLANES, LANES), F32)],
        compiler_params=_params("arbitrary", "arbitrary"),
        name="mlstm",
    )(qk, v, o_sig, cols, head_g, sel)


def _block_diag(band, kv_head):
    lane = lax.broadcasted_iota(jnp.int32, band.shape, 1)
    low = lane < ATTN_HEAD_DIM
    rolled = pltpu.roll(band, ATTN_HEAD_DIM, 1)
    if kv_head == 0:
        top = jnp.where(low, band, 0.0)
        bot = jnp.where(low, 0.0, rolled)
    else:
        top = jnp.where(low, rolled, 0.0)
        bot = jnp.where(low, 0.0, band)
    return jnp.concatenate([top, bot], axis=0).astype(BF16)


def _attn_kernel(q_ref, kc_ref, kp_ref, vc_ref, vp_ref, sink_ref, y_ref):
    W = WINDOW
    n_blocks = q_ref.shape[0] // W
    prev_penalty = jnp.where(pl.program_id(1) == 0, -jnp.inf, 0.0)

    key = lax.broadcasted_iota(jnp.int32, (W, PAIRS * W), 0)
    qry = lax.broadcasted_iota(jnp.int32, (W, PAIRS * W), 1) % W
    use_prev = key > qry

    def softmax_t(prev, cur, sink):
        comb = jnp.where(use_prev, prev, cur)
        m = jnp.maximum(jnp.max(comb, axis=0, keepdims=True), sink)
        p = jnp.exp(comb - m)
        den = jnp.sum(p, axis=0, keepdims=True) + jnp.exp(sink - m)
        p = p * (1.0 / den)
        return jnp.where(use_prev, p, 0.0), jnp.where(use_prev, 0.0, p)

    lanes = [[slice((kvh * PAIRS + p) * LANES, (kvh * PAIRS + p + 1) * LANES)
              for p in range(PAIRS)] for kvh in range(ATTN_KV_HEADS)]

    def band(ref, prev_ref, blk):
        prev = prev_ref[...] if blk == 0 else ref[(blk - 1) * W:blk * W, :]
        return jnp.concatenate([prev, ref[blk * W:(blk + 1) * W, :]], axis=0)

    group = ATTN_GROUP_BLOCKS if n_blocks % ATTN_GROUP_BLOCKS == 0 else 1
    for first in range(0, n_blocks, group):
        units = [(blk, kvh) for blk in range(first, first + group)
                 for kvh in range(ATTN_KV_HEADS)]
        st = {}
        for blk, kvh in units:
            rows = slice(blk * W, (blk + 1) * W)
            k_bd = _block_diag(band(kc_ref, kp_ref, blk), kvh)
            q_all = jnp.concatenate([q_ref[rows, ln] for ln in lanes[kvh]], axis=0)
            st[blk, kvh] = lax.dot_general(k_bd, q_all, (((1,), (1,)), ((), ())),
                                           preferred_element_type=F32)
        pt = {}
        for blk, kvh in units:
            prev_e, cur_e, prev_o, cur_o = (st[blk, kvh][i * W:(i + 1) * W, :] for i in range(4))
            if blk == 0:
                prev_e = prev_e + prev_penalty
                prev_o = prev_o + prev_penalty
            pe_prev, pe_cur = softmax_t(prev_e, cur_e, sink_ref[2 * kvh:2 * kvh + 1, :])
            po_prev, po_cur = softmax_t(prev_o, cur_o, sink_ref[2 * kvh + 1:2 * kvh + 2, :])
            pt[blk, kvh] = jnp.concatenate([pe_prev, pe_cur, po_prev, po_cur],
                                           axis=0).astype(BF16)
        for blk, kvh in units:
            rows = slice(blk * W, (blk + 1) * W)
            v_bd = _block_diag(band(vc_ref, vp_ref, blk), kvh)
            out_t = lax.dot_general(v_bd, pt[blk, kvh], (((0,), (0,)), ((), ())),
                                    preferred_element_type=F32)
            for p in range(PAIRS):
                y_ref[rows, lanes[kvh][p]] = out_t[:, p * W:(p + 1) * W].T.astype(BF16)


def _attention(l, qa, ka, va, sink_rows):
    b, s, _ = qa.shape
    tq = _pick_tile(s, TQ_ATTN)
    per = tq // WINDOW
    cur = lambda i, j: (i, j, 0)
    prev = lambda i, j: (i, jnp.maximum(j * per - 1, 0), 0)
    return pl.pallas_call(
        _attn_kernel,
        grid=(b, s // tq),
        in_specs=[
            pl.BlockSpec((None, tq, QA_W), cur),
            pl.BlockSpec((None, tq, KV_W), cur),
            pl.BlockSpec((None, WINDOW, KV_W), prev),
            pl.BlockSpec((None, tq, KV_W), cur),
            pl.BlockSpec((None, WINDOW, KV_W), prev),
            pl.BlockSpec((None, 2 * ATTN_KV_HEADS, PAIRS * WINDOW), lambda i, j: (l, 0, 0)),
        ],
        out_specs=pl.BlockSpec((None, tq, QA_W), cur),
        out_shape=jax.ShapeDtypeStruct((b, s, QA_W), BF16),
        compiler_params=_params("arbitrary", "arbitrary"),
        name="swa_attention",
    )(qa, ka, ka, va, va, sink_rows)


def _outproj_kernel(ym_ref, ya_ref, x_ref, gate_ref, w_ref, o_ref):
    km = ym_ref.shape[1]
    y = jnp.dot(ym_ref[...], w_ref[0:km, :], preferred_element_type=F32)
    y = y + jnp.dot(ya_ref[...], w_ref[km:, :], preferred_element_type=F32)
    o_ref[...] = x_ref[...] + gate_ref[...] * y


def _outproj(l, ym, ya, x, mod, w_out):
    b, s, d = x.shape
    tm = _pick_tile(s, TM_OUT)
    row = lambda i, j: (i, j, 0)
    return pl.pallas_call(
        _outproj_kernel,
        grid=(b, s // tm),
        in_specs=[
            pl.BlockSpec((None, tm, ym.shape[2]), row),
            pl.BlockSpec((None, tm, ya.shape[2]), row),
            pl.BlockSpec((None, tm, d), row),
            pl.BlockSpec((None, None, None, 1, d), lambda i, j: (l, i, 2, 0, 0)),
            pl.BlockSpec((None,) + w_out.shape[1:], lambda i, j: (l, 0, 0)),
        ],
        out_specs=pl.BlockSpec((None, tm, d), row),
        out_shape=jax.ShapeDtypeStruct((b, s, d), F32),
        compiler_params=_params("arbitrary", "arbitrary"),
        name="out_proj",
    )(ym, ya, x, mod, w_out)


def _shift_rows(x, prev_tail, k):
    rolled = pltpu.roll(x, k, 0)
    sub = lax.broadcasted_iota(jnp.int32, prev_tail.shape, 0)
    head = jnp.where(sub < k, pltpu.roll(prev_tail, k, 0), rolled[0:SUBLANES, :])
    return jnp.concatenate([head, rolled[SUBLANES:, :]], axis=0)


def _ffn_kernel(x_ref, shift_ref, scale_ref, gate_ref, g_ref, fg_ref, wg_ref, wu_ref, cw_ref, wd_ref,
                o_ref, hb_ref, gs0_ref, gs1_ref, us_ref, halo_ref, tail_ref, *, nf, final_norm):
    tm = x_ref.shape[0]
    halo = SUBLANES
    i = pl.program_id(1)
    j = pl.program_id(2)

    gs_slots = (gs0_ref, gs1_ref)
    tf = wg_ref.shape[1]
    n_up = tf // MXU_WIDTH if tf % MXU_WIDTH == 0 else 1
    upw = tf // n_up

    def up_g_piece(slot, n):
        cols = slice(n * upw, (n + 1) * upw)
        gs_slots[slot][:, cols] = jnp.dot(hb_ref[...], wg_ref[:, cols],
                                          preferred_element_type=F32)

    def up_u():
        us_ref[...] = jnp.dot(hb_ref[...], wu_ref[...], preferred_element_type=F32)

    def fill_halo():
        @pl.when(i == 0)
        def _():
            halo_ref[...] = jnp.zeros(halo_ref.shape, F32)

        @pl.when(i > 0)
        def _():
            halo_ref[...] = tail_ref[j - 1]

    n_rb = FFN_ROW_BLOCKS if tm % (FFN_ROW_BLOCKS * SUBLANES) == 0 else 1
    rb = tm // n_rb

    def gated_down(old, new=None):
        gs_ref = gs_slots[old]
        cw = cw_ref[...] * 0.5
        pending = list(range(n_up)) if new is not None else []
        for r in range(n_rb):
            if pending:
                up_g_piece(new, pending.pop(0))
            rows = slice(r * rb, (r + 1) * rb)
            g = gs_ref[rows, :]
            prev = halo_ref[...] if r == 0 else gs_ref[r * rb - halo:r * rb, :]
            half_c = _shift_rows(g, prev, 2) * cw[0:1, :]
            half_c = half_c + _shift_rows(g, prev, 1) * cw[1:2, :]
            half_c = half_c + g * cw[2:3, :]
            act = (half_c * (1.0 + jnp.tanh(half_c)) * us_ref[rows, :]).astype(BF16)
            o_ref[rows, :] += jnp.dot(act, wd_ref[...], preferred_element_type=F32)
        for n in pending:
            up_g_piece(new, n)
        tail_ref[j - 1] = gs_ref[tm - halo:tm, :]

    @pl.when(j == 0)
    def _():
        hb_ref[...] = _adaln(x_ref[...], g_ref[...], shift_ref[...], scale_ref[...]).astype(BF16)
        o_ref[...] = jnp.zeros(o_ref.shape, F32)
        for n in range(n_up):
            up_g_piece(0, n)
        up_u()

    for par in range(2):
        @pl.when((j > 0) & (j < nf) & (lax.rem(j, 2) == par))
        def _():
            fill_halo()
            gated_down(1 - par, par)
            up_u()

    @pl.when(j == nf)
    def _():
        fill_halo()
        gated_down((nf - 1) % 2)
        res = x_ref[...] + gate_ref[...] * o_ref[...]
        if final_norm:
            ms = jnp.mean(res * res, axis=-1, keepdims=True)
            res = res * lax.rsqrt(ms + NORM_EPS) * fg_ref[...]
        o_ref[...] = res


def _ffn(l, x, mod, g, final_g, w_up, conv_ffn, w_down, final_norm):
    b, s, d = x.shape
    ff = w_down.shape[1]
    tm, tf = _ffn_tiles(s, ff)
    nf = ff // tf
    row = lambda i, t, j: (i, t, 0)
    mod_spec = lambda k: pl.BlockSpec((None, None, None, 1, d), lambda i, t, j: (l, i, k, 0, 0))
    up_tile = lambda j: jnp.minimum(j, nf - 1)
    down_tile = lambda j: jnp.maximum(j - 1, 0)
    return pl.pallas_call(
        functools.partial(_ffn_kernel, nf=nf, final_norm=final_norm),
        grid=(b, s // tm, nf + 1),
        in_specs=[
            pl.BlockSpec((None, tm, d), row),
            mod_spec(3), mod_spec(4), mod_spec(5),
            pl.BlockSpec((None, 1, d), lambda i, t, j: (l, 0, 0)),
            pl.BlockSpec((1, d), lambda i, t, j: (0, 0)),
            pl.BlockSpec((None, d, tf), lambda i, t, j: (l, 0, up_tile(j))),
            pl.BlockSpec((None, d, tf), lambda i, t, j: (l, 0, nf + up_tile(j))),
            pl.BlockSpec((None, FFN_CONV, tf), lambda i, t, j: (l, 0, down_tile(j))),
            pl.BlockSpec((None, tf, d), lambda i, t, j: (l, down_tile(j), 0)),
        ],
        out_specs=pl.BlockSpec((None, tm, d), row),
        out_shape=jax.ShapeDtypeStruct((b, s, d), F32),
        scratch_shapes=[pltpu.VMEM((tm, d), BF16),
                        pltpu.VMEM((tm, tf), F32),
                        pltpu.VMEM((tm, tf), F32),
                        pltpu.VMEM((tm, tf), F32),
                        pltpu.VMEM((SUBLANES, tf), F32),
                        pltpu.VMEM((nf, SUBLANES, tf), F32)],
        compiler_params=_params("arbitrary", "arbitrary", "arbitrary"),
        name="convglu_ffn",
    )(x, mod, mod, mod, g, final_g, w_up, w_up, conv_ffn, w_down)


def _pack_w_in(w_in):
    nh = MLSTM_HEADS
    o_gates = QK_W + 2 * V_W
    o_attn = o_gates + 2 * nh
    w_in = w_in.astype(BF16)
    gates = w_in[:, :, o_gates:o_attn]
    gates = jnp.pad(gates, ((0, 0), (0, 0), (0, LANES - 2 * nh)))
    return jnp.concatenate([w_in[:, :, :o_gates], w_in[:, :, o_attn:], gates], axis=-1)


def _sink_rows(sinks):
    depth = sinks.shape[0]
    t = sinks.reshape(depth, ATTN_KV_HEADS, PAIRS, 2).transpose(0, 1, 3, 2)
    t = jnp.repeat(t[..., None], WINDOW, axis=-1)
    return t.reshape(depth, 2 * ATTN_KV_HEADS, PAIRS * WINDOW)


def kernel(x, c, positions, ada_w, ada_b, norm_mix_g, w_in, b_gates, conv_qk, mlstm_head_g, sinks,
           w_out, norm_ffn_g, w_up, conv_ffn, w_down, final_g):
    b, s, d = x.shape
    depth = ada_w.shape[0]
    nh = MLSTM_HEADS

    c_pad = jnp.pad(c, ((0, 2 * SUBLANES - b), (0, 0)))
    mod = _modulation(c_pad, ada_w, ada_b)[:, :b].reshape(depth, b, 6, 1, d)
    cos, sina, sinb = _rope_tables(positions)

    w_in_p = _pack_w_in(w_in)
    w_out_b = w_out.astype(BF16)
    w_up_b = w_up.astype(BF16)
    w_down_b = w_down.astype(BF16)
    bg = jnp.pad(b_gates, ((0, 0), (0, LANES - 2 * nh))).reshape(depth, 1, LANES)
    norm_mix = norm_mix_g.reshape(depth, 1, d)
    norm_ffn = norm_ffn_g.reshape(depth, 1, d)
    head_g = mlstm_head_g.reshape(depth, 1, V_W)
    sink_rows = _sink_rows(sinks)
    gate_sel = _gate_selector()
    final_g2 = final_g.reshape(1, d)

    for l in range(depth):
        qk, v, o_sig, cols, qa, ka, va = _inproj(
            l, x, mod, norm_mix, w_in_p, conv_qk, bg, cos, sina, sinb)
        ym = _mlstm(l, qk, v, o_sig, cols, head_g, gate_sel)
        ya = _attention(l, qa, ka, va, sink_rows)
        x = _outproj(l, ym, ya, x, mod, w_out_b)
        x = _ffn(l, x, mod, norm_ffn, final_g2, w_up_b, conv_ffn, w_down_b,
                 final_norm=(l == depth - 1))
    return x
```

```python
import functools

import jax
import jax.numpy as jnp
from jax import lax
from jax.experimental import pallas as pl
from jax.experimental.pallas import tpu as pltpu

F32 = jnp.float32
BF16 = jnp.bfloat16

MLSTM_HEADS = 4
MLSTM_DQK = 128
MLSTM_DV = 256
MLSTM_CONV = 4
MLSTM_CHUNK = 256
GATE_SOFTCAP = 15.0
ATTN_HEAD_DIM = 64
ATTN_HEADS = 16
ATTN_KV_HEADS = 2
WINDOW = 128
ROPE_DIM = 16
ROPE_THETA = 500000.0
FFN_CONV = 3
FFN_ROW_BLOCKS = 4
NORM_EPS = 1e-6

LANES = 128
SUBLANES = 8
MXU_WIDTH = 256
VMEM_LIMIT = 62 * 1024 * 1024

TM_PROJ = 512
TM_OUT = 1024
TQ_ATTN = 1024
ATTN_GROUP_BLOCKS = 4
MLSTM_STEP_CHUNKS = 4
TM_FFN = 1024
TF_FFN = 512
TN_MOD = 1024
TS_ROPE = 1024

QK_W = 2 * MLSTM_HEADS * MLSTM_DQK
V_W = MLSTM_HEADS * MLSTM_DV
QA_W = ATTN_HEADS * ATTN_HEAD_DIM
KV_W = ATTN_KV_HEADS * ATTN_HEAD_DIM
OFF_QK = 0
OFF_V = OFF_QK + QK_W
OFF_O = OFF_V + V_W
OFF_QA = OFF_O + V_W
OFF_KA = OFF_QA + QA_W
OFF_VA = OFF_KA + KV_W
OFF_G = OFF_VA + KV_W
IN_PACKED = OFF_G + LANES

COL_G = 0
COL_B = MLSTM_HEADS
COL_CG = 2 * MLSTM_HEADS
PAIRS = ATTN_HEADS // ATTN_KV_HEADS // 2


def _params(*sem):
    return pltpu.CompilerParams(dimension_semantics=sem, vmem_limit_bytes=VMEM_LIMIT)


def _pick_tile(n, pref):
    t = min(n, pref)
    while n % t:
        t //= 2
    return t


def _ffn_tiles(s, ff):
    return _pick_tile(s, TM_FFN), _pick_tile(ff, TF_FFN)


def _mod_kernel(c_ref, w_ref, b_ref, o_ref):
    ca = jax.nn.silu(c_ref[...]).astype(BF16)
    o_ref[...] = jnp.dot(ca, w_ref[...].astype(BF16), preferred_element_type=F32) + b_ref[...]


def _modulation(c_pad, ada_w, ada_b):
    depth, d, n = ada_w.shape
    rows = c_pad.shape[0]
    tn = _pick_tile(n, TN_MOD)
    return pl.pallas_call(
        _mod_kernel,
        grid=(depth, n // tn),
        in_specs=[
            pl.BlockSpec((rows, d), lambda l, j: (0, 0)),
            pl.BlockSpec((None, d, tn), lambda l, j: (l, 0, j)),
            pl.BlockSpec((None, 1, tn), lambda l, j: (l, 0, j)),
        ],
        out_specs=pl.BlockSpec((None, rows, tn), lambda l, j: (l, 0, j)),
        out_shape=jax.ShapeDtypeStruct((depth, rows, n), F32),
        compiler_params=_params("arbitrary", "arbitrary"),
        name="adaln_mod",
    )(c_pad, ada_w, ada_b.reshape(depth, 1, n))


def _rope_kernel(pos_ref, freq_ref, cos_ref, sina_ref, sinb_ref):
    ang = pos_ref[...].astype(F32) * freq_ref[...]
    d = lax.broadcasted_iota(jnp.int32, ang.shape, 1) % ATTN_HEAD_DIM
    c = jnp.cos(ang)
    s = jnp.sin(ang)
    half = ROPE_DIM // 2
    cos_ref[...] = jnp.where(d < ROPE_DIM, c, 1.0)
    sina_ref[...] = jnp.where(d < half, -s, 0.0)
    sinb_ref[...] = jnp.where((d >= half) & (d < ROPE_DIM), s, 0.0)


def _rope_tables(positions):
    b, s = positions.shape
    ts = _pick_tile(s, TS_ROPE)
    inv_freq = ROPE_THETA ** (-jnp.arange(0, ROPE_DIM, 2, dtype=F32) / ROPE_DIM)
    per_head = jnp.concatenate(
        [inv_freq, inv_freq, jnp.zeros((ATTN_HEAD_DIM - ROPE_DIM,), F32)])
    freq = jnp.tile(per_head, LANES // ATTN_HEAD_DIM).reshape(1, LANES)
    tab = jax.ShapeDtypeStruct((b, s, LANES), F32)
    spec = pl.BlockSpec((None, ts, LANES), lambda i, j: (i, j, 0))
    return pl.pallas_call(
        _rope_kernel,
        grid=(b, s // ts),
        in_specs=[pl.BlockSpec((None, ts, 1), lambda i, j: (i, j, 0)),
                  pl.BlockSpec((1, LANES), lambda i, j: (0, 0))],
        out_specs=[spec, spec, spec],
        out_shape=[tab, tab, tab],
        compiler_params=_params("arbitrary", "arbitrary"),
        name="rope_tables",
    )(positions.reshape(b, s, 1), freq)


def _adaln(x, g, shift, scale):
    ms = jnp.mean(x * x, axis=-1, keepdims=True)
    h = x * lax.rsqrt(ms + NORM_EPS) * g
    return h * (1.0 + scale) + shift


def _rope(t, cos, sina, sinb):
    n = t.shape[-1]
    half = ROPE_DIM // 2
    return t * cos + pltpu.roll(t, n - half, 1) * sina + pltpu.roll(t, half, 1) * sinb


def _chunk_scan_rows(x, op, fill):
    pos = lax.broadcasted_iota(jnp.int32, x.shape, 0) % MLSTM_CHUNK
    step = 1
    while step < MLSTM_CHUNK:
        x = op(x, jnp.where(pos >= step, pltpu.roll(x, step, 0), fill))
        step *= 2
    return x


def _gate_columns(gates):
    nh = MLSTM_HEADS
    lane = lax.broadcasted_iota(jnp.int32, gates.shape, 1)
    b = _chunk_scan_rows(jax.nn.log_sigmoid(gates), jnp.add, 0.0)
    g = gates - pltpu.roll(b, LANES - nh, 1)
    cg = _chunk_scan_rows(g, jnp.maximum, -jnp.inf)
    return jnp.where(lane < COL_B, g, jnp.where(lane < COL_CG, b, pltpu.roll(cg, COL_CG, 1)))


def _inproj_kernel(x_ref, shift_ref, scale_ref, g_ref, w_ref, cw_ref, bg_ref,
                   cos_ref, sina_ref, sinb_ref,
                   qk_ref, v_ref, o_ref, col_ref, qa_ref, ka_ref, va_ref,
                   hb_ref, cbuf_ref):
    tm = x_ref.shape[0]
    halo = SUBLANES

    @pl.when(pl.program_id(1) == 0)
    def _():
        cbuf_ref[...] = jnp.zeros(cbuf_ref.shape, F32)

    hb_ref[...] = _adaln(x_ref[...], g_ref[...], shift_ref[...], scale_ref[...]).astype(BF16)

    def proj(off, width):
        return jnp.dot(hb_ref[...], w_ref[:, off:off + width], preferred_element_type=F32)

    cw = cw_ref[...]
    n_parts = 4
    part = QK_W // n_parts
    for c in range(n_parts):
        lo = c * part
        pq = proj(OFF_QK + lo, part)
        prev = cbuf_ref[:, lo:lo + part]
        acc = _shift_rows(pq, prev, 3) * cw[0:1, lo:lo + part]
        acc = acc + _shift_rows(pq, prev, 2) * cw[1:2, lo:lo + part]
        acc = acc + _shift_rows(pq, prev, 1) * cw[2:3, lo:lo + part]
        acc = acc + pq * cw[3:4, lo:lo + part]
        cbuf_ref[:, lo:lo + part] = pq[tm - halo:tm, :]
        act = jax.nn.silu(acc)
        if lo < QK_W // 2:
            act = act * (MLSTM_DQK ** -0.5)
        qk_ref[:, lo:lo + part] = act.astype(BF16)
        o_ref[:, lo:lo + part] = jax.nn.sigmoid(proj(OFF_O + lo, part))

    pg = proj(OFF_G, LANES) + bg_ref[...]
    col_ref[...] = _gate_columns(GATE_SOFTCAP * jnp.tanh(pg / GATE_SOFTCAP))

    cos = cos_ref[...]
    sina = sina_ref[...]
    sinb = sinb_ref[...]
    for c in range(QA_W // MXU_WIDTH):
        pa = proj(OFF_QA + c * MXU_WIDTH, MXU_WIDTH)
        for k in range(MXU_WIDTH // LANES):
            lo = c * MXU_WIDTH + k * LANES
            r = _rope(pa[:, k * LANES:(k + 1) * LANES], cos, sina, sinb)
            qa_ref[:, lo:lo + LANES] = (r * (ATTN_HEAD_DIM ** -0.5)).astype(BF16)

    pkv = proj(OFF_KA, 2 * KV_W)
    ka_ref[...] = _rope(pkv[:, 0:KV_W], cos, sina, sinb)
    va_ref[...] = pkv[:, KV_W:2 * KV_W]

    for c in range(2):
        lo = c * (V_W // 2)
        v_ref[:, lo:lo + V_W // 2] = proj(OFF_V + lo, V_W // 2).astype(BF16)


def _inproj(l, x, mod, g, w_packed, conv_qk, bg, cos, sina, sinb):
    b, s, d = x.shape
    tm = _pick_tile(s, TM_PROJ)
    row = lambda i, j: (i, j, 0)
    lay = lambda i, j: (l, 0, 0)
    outs = [
        (QK_W, BF16), (V_W, BF16), (V_W, F32), (LANES, F32), (QA_W, BF16), (KV_W, F32), (KV_W, F32),
    ]
    return pl.pallas_call(
        _inproj_kernel,
        grid=(b, s // tm),
        in_specs=[
            pl.BlockSpec((None, tm, d), row),
            pl.BlockSpec((None, None, None, 1, d), lambda i, j: (l, i, 0, 0, 0)),
            pl.BlockSpec((None, None, None, 1, d), lambda i, j: (l, i, 1, 0, 0)),
            pl.BlockSpec((None, 1, d), lay),
            pl.BlockSpec((None, d, IN_PACKED), lay),
            pl.BlockSpec((None, MLSTM_CONV, QK_W), lay),
            pl.BlockSpec((None, 1, LANES), lay),
            pl.BlockSpec((None, tm, LANES), row),
            pl.BlockSpec((None, tm, LANES), row),
            pl.BlockSpec((None, tm, LANES), row),
        ],
        out_specs=[pl.BlockSpec((None, tm, w), row) for w, _ in outs],
        out_shape=[jax.ShapeDtypeStruct((b, s, w), dt) for w, dt in outs],
        scratch_shapes=[pltpu.VMEM((tm, d), BF16), pltpu.VMEM((SUBLANES, QK_W), F32)],
        compiler_params=_params("arbitrary", "arbitrary"),
        name="in_proj",
    )(x, mod, mod, g, w_packed, conv_qk, bg, cos, sina, sinb)


def _gate_selector():
    nh = MLSTM_HEADS
    src = jnp.arange(LANES)[:, None]
    group = jnp.arange(3 * nh * LANES)[None, :] // LANES
    base = jnp.asarray([COL_CG, COL_G, COL_B])[group % 3]
    return (src == base + group // 3).astype(BF16)


def _mlstm_kernel(qk_ref, v_ref, o_ref, col_ref, hg_ref, sel_ref, y_ref, c_ref, m_ref):
    L = MLSTM_CHUNK
    nh = MLSTM_HEADS
    n_chunks = qk_ref.shape[0] // L

    @pl.when(pl.program_id(1) == 0)
    def _():
        c_ref[...] = jnp.zeros(c_ref.shape, F32)
        m_ref[...] = jnp.zeros(m_ref.shape, F32)

    ri = lax.broadcasted_iota(jnp.int32, (L, L), 0)
    ci = lax.broadcasted_iota(jnp.int32, (L, L), 1)
    causal = ci <= ri
    ones_col = (lax.broadcasted_iota(jnp.int32, (L, LANES), 1) == 0).astype(BF16)
    nt = (((1,), (1,)), ((), ()))
    tn = (((0,), (0,)), ((), ()))
    units = [(c, h) for c in range(n_chunks) for h in range(nh)]
    rows = [slice(c * L, (c + 1) * L) for c in range(n_chunks)]
    dvs = [slice(h * MLSTM_DV, (h + 1) * MLSTM_DV) for h in range(nh)]

    col = [col_ref[r, :] for r in rows]
    col_t = [cl.T for cl in col]
    q = {(c, h): qk_ref[rows[c], h * MLSTM_DQK:(h + 1) * MLSTM_DQK] for c, h in units}
    k = {(c, h): qk_ref[rows[c], (nh + h) * MLSTM_DQK:(nh + h + 1) * MLSTM_DQK] for c, h in units}
    v_ext = {(c, h): jnp.concatenate([v_ref[rows[c], dvs[h]], ones_col], axis=1) for c, h in units}

    def replicate(x):
        hi = x.astype(BF16)
        rem = x - hi.astype(F32)
        mid = rem.astype(BF16)
        lo = (rem - mid.astype(F32)).astype(BF16)
        sel = sel_ref[...]
        return ((jnp.dot(hi, sel, preferred_element_type=F32)
                 + jnp.dot(mid, sel, preferred_element_type=F32))
                + jnp.dot(lo, sel, preferred_element_type=F32))

    def twice(x):
        return jnp.concatenate([x, x], axis=1)

    rep = [replicate(cl) for cl in col]

    def gate(c, h, kind):
        lo = (3 * h + kind) * LANES
        return rep[c][:, lo:lo + LANES]

    mm, w_inter, decay, w_state = {}, {}, {}, {}
    for h in range(nh):
        m = m_ref[h:h + 1, :]
        for c in range(n_chunks):
            mm[c, h] = jnp.maximum(m, gate(c, h, 0))
            mm_last = mm[c, h][L - 1:L, :]
            w_inter[c, h] = jnp.exp(m - mm[c, h])
            decay[c, h] = jnp.exp(m - mm_last)
            w_state[c, h] = jnp.exp(gate(c, h, 1) - mm_last)
            m = gate(c, h, 2)[L - 1:L, :] + mm_last
        m_ref[h:h + 1, :] = m

    s = {}
    for u in units:
        c, h = u
        g_row = col_t[c][COL_G + h:COL_G + h + 1, :]
        w_intra = jnp.exp(jnp.where(causal, g_row - twice(mm[u]), -jnp.inf))
        s[u] = lax.dot_general(q[u], k[u], nt, preferred_element_type=F32) * w_intra
    intra = {u: jnp.dot(s[u].astype(BF16), v_ext[u][:, 0:MLSTM_DV], preferred_element_type=F32)
             for u in units}
    upd = {u: lax.dot_general((k[u].astype(F32) * w_state[u]).astype(BF16), v_ext[u], tn,
                              preferred_element_type=F32) for u in units}

    c_in = {}
    for h in range(nh):
        state = c_ref[h]
        for c in range(n_chunks):
            c_in[c, h] = state.astype(BF16)
            d = decay[c, h]
            state = jnp.concatenate([d, d, d], axis=1) * state + upd[c, h]
        c_ref[h] = state

    def lane_sum(x):
        return jnp.sum(x[:, 0:LANES] + x[:, LANES:2 * LANES], axis=-1, keepdims=True)

    for u in units:
        c, h = u
        inter = jnp.dot(q[u], c_in[u], preferred_element_type=F32)
        num = intra[u] + twice(w_inter[u]) * inter[:, 0:MLSTM_DV]
        nq = lane_sum(s[u]) + w_inter[u][:, 0:1] * inter[:, MLSTM_DV:MLSTM_DV + 1]
        m_t = gate(c, h, 2)[:, 0:1] + mm[u][:, 0:1]
        rden = 1.0 / jnp.maximum(jnp.abs(nq), jnp.exp(-m_t))
        msq = lane_sum(num * num) * (1.0 / MLSTM_DV)
        scale = rden * lax.rsqrt(rden * rden * msq + NORM_EPS)
        hn = num * scale * hg_ref[:, dvs[h]]
        y_ref[rows[c], dvs[h]] = (o_ref[rows[c], dvs[h]] * hn).astype(BF16)


def _mlstm(l, qk, v, o_sig, cols, head_g, sel):
    b, s, _ = qk.shape
    assert s % MLSTM_CHUNK == 0, (s, MLSTM_CHUNK)
    t = _pick_tile(s // MLSTM_CHUNK, MLSTM_STEP_CHUNKS) * MLSTM_CHUNK
    row = lambda i, j: (i, j, 0)
    return pl.pallas_call(
        _mlstm_kernel,
        grid=(b, s // t),
        in_specs=[
            pl.BlockSpec((None, t, QK_W), row),
            pl.BlockSpec((None, t, V_W), row),
            pl.BlockSpec((None, t, V_W), row),
            pl.BlockSpec((None, t, LANES), row),
            pl.BlockSpec((None, 1, V_W), lambda i, j: (l, 0, 0)),
            pl.BlockSpec(sel.shape, lambda i, j: (0, 0)),
        ],
        out_specs=pl.BlockSpec((None, t, V_W), row),
        out_shape=jax.ShapeDtypeStruct((b, s, V_W), BF16),
        scratch_shapes=[pltpu.VMEM((MLSTM_HEADS, MLSTM_DQK, MLSTM_DV + LANES), F32),
                        pltpu.VMEM((SUBLANES, LANES), F32)],
        compiler_params=_params("arbitrary", "arbitrary"),
        name="mlstm",
    )(qk, v, o_sig, cols, head_g, sel)


def _block_diag(band, kv_head):
    lane = lax.broadcasted_iota(jnp.int32, band.shape, 1)
    low = lane < ATTN_HEAD_DIM
    rolled = pltpu.roll(band, ATTN_HEAD_DIM, 1)
    if kv_head == 0:
        top = jnp.where(low, band, 0.0)
        bot = jnp.where(low, 0.0, rolled)
    else:
        top = jnp.where(low, rolled, 0.0)
        bot = jnp.where(low, 0.0, band)
    return jnp.concatenate([top, bot], axis=0).astype(BF16)


def _attn_kernel(q_ref, kc_ref, kp_ref, vc_ref, vp_ref, sink_ref, y_ref):
    W = WINDOW
    n_blocks = q_ref.shape[0] // W
    prev_penalty = jnp.where(pl.program_id(1) == 0, -jnp.inf, 0.0)

    key = lax.broadcasted_iota(jnp.int32, (W, PAIRS * W), 0)
    qry = lax.broadcasted_iota(jnp.int32, (W, PAIRS * W), 1) % W
    use_prev = key > qry

    def softmax_t(prev, cur, sink):
        comb = jnp.where(use_prev, prev, cur)
        m = jnp.maximum(jnp.max(comb, axis=0, keepdims=True), sink)
        p = jnp.exp(comb - m)
        den = jnp.sum(p, axis=0, keepdims=True) + jnp.exp(sink - m)
        p = p * (1.0 / den)
        return jnp.where(use_prev, p, 0.0), jnp.where(use_prev, 0.0, p)

    lanes = [[slice((kvh * PAIRS + p) * LANES, (kvh * PAIRS + p + 1) * LANES)
              for p in range(PAIRS)] for kvh in range(ATTN_KV_HEADS)]

    def band(ref, prev_ref, blk):
        prev = prev_ref[...] if blk == 0 else ref[(blk - 1) * W:blk * W, :]
        return jnp.concatenate([prev, ref[blk * W:(blk + 1) * W, :]], axis=0)

    group = ATTN_GROUP_BLOCKS if n_blocks % ATTN_GROUP_BLOCKS == 0 else 1
    for first in range(0, n_blocks, group):
        units = [(blk, kvh) for blk in range(first, first + group)
                 for kvh in range(ATTN_KV_HEADS)]
        st = {}
        for blk, kvh in units:
            rows = slice(blk * W, (blk + 1) * W)
            k_bd = _block_diag(band(kc_ref, kp_ref, blk), kvh)
            q_all = jnp.concatenate([q_ref[rows, ln] for ln in lanes[kvh]], axis=0)
            st[blk, kvh] = lax.dot_general(k_bd, q_all, (((1,), (1,)), ((), ())),
                                           preferred_element_type=F32)
        pt = {}
        for blk, kvh in units:
            prev_e, cur_e, prev_o, cur_o = (st[blk, kvh][i * W:(i + 1) * W, :] for i in range(4))
            if blk == 0:
                prev_e = prev_e + prev_penalty
                prev_o = prev_o + prev_penalty
            pe_prev, pe_cur = softmax_t(prev_e, cur_e, sink_ref[2 * kvh:2 * kvh + 1, :])
            po_prev, po_cur = softmax_t(prev_o, cur_o, sink_ref[2 * kvh + 1:2 * kvh + 2, :])
            pt[blk, kvh] = jnp.concatenate([pe_prev, pe_cur, po_prev, po_cur],
                                           axis=0).astype(BF16)
        for blk, kvh in units:
            rows = slice(blk * W, (blk + 1) * W)
            v_bd = _block_diag(band(vc_ref, vp_ref, blk), kvh)
            out_t = lax.dot_general(v_bd, pt[blk, kvh], (((0,), (0,)), ((), ())),
                                    preferred_element_type=F32)
            for p in range(PAIRS):
                y_ref[rows, lanes[kvh][p]] = out_t[:, p * W:(p + 1) * W].T.astype(BF16)


def _attention(l, qa, ka, va, sink_rows):
    b, s, _ = qa.shape
    tq = _pick_tile(s, TQ_ATTN)
    per = tq // WINDOW
    cur = lambda i, j: (i, j, 0)
    prev = lambda i, j: (i, jnp.maximum(j * per - 1, 0), 0)
    return pl.pallas_call(
        _attn_kernel,
        grid=(b, s // tq),
        in_specs=[
            pl.BlockSpec((None, tq, QA_W), cur),
            pl.BlockSpec((None, tq, KV_W), cur),
            pl.BlockSpec((None, WINDOW, KV_W), prev),
            pl.BlockSpec((None, tq, KV_W), cur),
            pl.BlockSpec((None, WINDOW, KV_W), prev),
            pl.BlockSpec((None, 2 * ATTN_KV_HEADS, PAIRS * WINDOW), lambda i, j: (l, 0, 0)),
        ],
        out_specs=pl.BlockSpec((None, tq, QA_W), cur),
        out_shape=jax.ShapeDtypeStruct((b, s, QA_W), BF16),
        compiler_params=_params("arbitrary", "arbitrary"),
        name="swa_attention",
    )(qa, ka, ka, va, va, sink_rows)


def _outproj_kernel(ym_ref, ya_ref, x_ref, gate_ref, w_ref, o_ref):
    km = ym_ref.shape[1]
    y = jnp.dot(ym_ref[...], w_ref[0:km, :], preferred_element_type=F32)
    y = y + jnp.dot(ya_ref[...], w_ref[km:, :], preferred_element_type=F32)
    o_ref[...] = x_ref[...] + gate_ref[...] * y


def _outproj(l, ym, ya, x, mod, w_out):
    b, s, d = x.shape
    tm = _pick_tile(s, TM_OUT)
    row = lambda i, j: (i, j, 0)
    return pl.pallas_call(
        _outproj_kernel,
        grid=(b, s // tm),
        in_specs=[
            pl.BlockSpec((None, tm, ym.shape[2]), row),
            pl.BlockSpec((None, tm, ya.shape[2]), row),
            pl.BlockSpec((None, tm, d), row),
            pl.BlockSpec((None, None, None, 1, d), lambda i, j: (l, i, 2, 0, 0)),
            pl.BlockSpec((None,) + w_out.shape[1:], lambda i, j: (l, 0, 0)),
        ],
        out_specs=pl.BlockSpec((None, tm, d), row),
        out_shape=jax.ShapeDtypeStruct((b, s, d), F32),
        compiler_params=_params("arbitrary", "arbitrary"),
        name="out_proj",
    )(ym, ya, x, mod, w_out)


def _shift_rows(x, prev_tail, k):
    rolled = pltpu.roll(x, k, 0)
    sub = lax.broadcasted_iota(jnp.int32, prev_tail.shape, 0)
    head = jnp.where(sub < k, pltpu.roll(prev_tail, k, 0), rolled[0:SUBLANES, :])
    return jnp.concatenate([head, rolled[SUBLANES:, :]], axis=0)


def _ffn_kernel(x_ref, shift_ref, scale_ref, gate_ref, g_ref, fg_ref, wg_ref, wu_ref, cw_ref, wd_ref,
                o_ref, hb_ref, gs0_ref, gs1_ref, us0_ref, us1_ref, halo_ref, tail_ref, *, nf,
                final_norm):
    tm = x_ref.shape[0]
    halo = SUBLANES
    i = pl.program_id(1)
    j = pl.program_id(2)

    gs_slots = (gs0_ref, gs1_ref)
    tf = wg_ref.shape[1]
    n_up = tf // MXU_WIDTH if tf % MXU_WIDTH == 0 else 1
    upw = tf // n_up

    def up_g_piece(slot, n):
        cols = slice(n * upw, (n + 1) * upw)
        gs_slots[slot][:, cols] = jnp.dot(hb_ref[...], wg_ref[:, cols],
                                          preferred_element_type=F32)

    us_slots = (us0_ref, us1_ref)

    def up_u_piece(slot, n):
        cols = slice(n * upw, (n + 1) * upw)
        us_slots[slot][:, cols] = jnp.dot(hb_ref[...], wu_ref[:, cols],
                                          preferred_element_type=F32)

    def fill_halo():
        @pl.when(i == 0)
        def _():
            halo_ref[...] = jnp.zeros(halo_ref.shape, F32)

        @pl.when(i > 0)
        def _():
            halo_ref[...] = tail_ref[j - 1]

    n_rb = FFN_ROW_BLOCKS if tm % (FFN_ROW_BLOCKS * SUBLANES) == 0 else 1
    rb = tm // n_rb

    def gated_down(old, new=None):
        gs_ref = gs_slots[old]
        us_ref = us_slots[old]
        cw = cw_ref[...] * 0.5
        pending = []
        if new is not None:
            for n in range(n_up):
                pending += [functools.partial(up_g_piece, new, n),
                            functools.partial(up_u_piece, new, n)]
        for r in range(n_rb):
            if pending:
                pending.pop(0)()
            rows = slice(r * rb, (r + 1) * rb)
            g = gs_ref[rows, :]
            prev = halo_ref[...] if r == 0 else gs_ref[r * rb - halo:r * rb, :]
            half_c = _shift_rows(g, prev, 2) * cw[0:1, :]
            half_c = half_c + _shift_rows(g, prev, 1) * cw[1:2, :]
            half_c = half_c + g * cw[2:3, :]
            act = (half_c * (1.0 + jnp.tanh(half_c)) * us_ref[rows, :]).astype(BF16)
            o_ref[rows, :] += jnp.dot(act, wd_ref[...], preferred_element_type=F32)
        for piece in pending:
            piece()
        tail_ref[j - 1] = gs_ref[tm - halo:tm, :]

    @pl.when(j == 0)
    def _():
        hb_ref[...] = _adaln(x_ref[...], g_ref[...], shift_ref[...], scale_ref[...]).astype(BF16)
        o_ref[...] = jnp.zeros(o_ref.shape, F32)
        for n in range(n_up):
            up_g_piece(0, n)
            up_u_piece(0, n)

    for par in range(2):
        @pl.when((j > 0) & (j < nf) & (lax.rem(j, 2) == par))
        def _():
            fill_halo()
            gated_down(1 - par, par)

    @pl.when(j == nf)
    def _():
        fill_halo()
        gated_down((nf - 1) % 2)
        res = x_ref[...] + gate_ref[...] * o_ref[...]
        if final_norm:
            ms = jnp.mean(res * res, axis=-1, keepdims=True)
            res = res * lax.rsqrt(ms + NORM_EPS) * fg_ref[...]
        o_ref[...] = res


def _ffn(l, x, mod, g, final_g, w_up, conv_ffn, w_down, final_norm):
    b, s, d = x.shape
    ff = w_down.shape[1]
    tm, tf = _ffn_tiles(s, ff)
    nf = ff // tf
    row = lambda i, t, j: (i, t, 0)
    mod_spec = lambda k: pl.BlockSpec((None, None, None, 1, d), lambda i, t, j: (l, i, k, 0, 0))
    up_tile = lambda j: jnp.minimum(j, nf - 1)
    down_tile = lambda j: jnp.maximum(j - 1, 0)
    return pl.pallas_call(
        functools.partial(_ffn_kernel, nf=nf, final_norm=final_norm),
        grid=(b, s // tm, nf + 1),
        in_specs=[
            pl.BlockSpec((None, tm, d), row),
            mod_spec(3), mod_spec(4), mod_spec(5),
            pl.BlockSpec((None, 1, d), lambda i, t, j: (l, 0, 0)),
            pl.BlockSpec((1, d), lambda i, t, j: (0, 0)),
            pl.BlockSpec((None, d, tf), lambda i, t, j: (l, 0, up_tile(j))),
            pl.BlockSpec((None, d, tf), lambda i, t, j: (l, 0, nf + up_tile(j))),
            pl.BlockSpec((None, FFN_CONV, tf), lambda i, t, j: (l, 0, down_tile(j))),
            pl.BlockSpec((None, tf, d), lambda i, t, j: (l, down_tile(j), 0)),
        ],
        out_specs=pl.BlockSpec((None, tm, d), row),
        out_shape=jax.ShapeDtypeStruct((b, s, d), F32),
        scratch_shapes=[pltpu.VMEM((tm, d), BF16),
                        pltpu.VMEM((tm, tf), F32),
                        pltpu.VMEM((tm, tf), F32),
                        pltpu.VMEM((tm, tf), F32),
                        pltpu.VMEM((tm, tf), F32),
                        pltpu.VMEM((SUBLANES, tf), F32),
                        pltpu.VMEM((nf, SUBLANES, tf), F32)],
        compiler_params=_params("arbitrary", "arbitrary", "arbitrary"),
        name="convglu_ffn",
    )(x, mod, mod, mod, g, final_g, w_up, w_up, conv_ffn, w_down)


def _pack_w_in(w_in):
    nh = MLSTM_HEADS
    o_gates = QK_W + 2 * V_W
    o_attn = o_gates + 2 * nh
    w_in = w_in.astype(BF16)
    gates = w_in[:, :, o_gates:o_attn]
    gates = jnp.pad(gates, ((0, 0), (0, 0), (0, LANES - 2 * nh)))
    return jnp.concatenate([w_in[:, :, :o_gates], w_in[:, :, o_attn:], gates], axis=-1)


def _sink_rows(sinks):
    depth = sinks.shape[0]
    t = sinks.reshape(depth, ATTN_KV_HEADS, PAIRS, 2).transpose(0, 1, 3, 2)
    t = jnp.repeat(t[..., None], WINDOW, axis=-1)
    return t.reshape(depth, 2 * ATTN_KV_HEADS, PAIRS * WINDOW)


def kernel(x, c, positions, ada_w, ada_b, norm_mix_g, w_in, b_gates, conv_qk, mlstm_head_g, sinks,
           w_out, norm_ffn_g, w_up, conv_ffn, w_down, final_g):
    b, s, d = x.shape
    depth = ada_w.shape[0]
    nh = MLSTM_HEADS

    c_pad = jnp.pad(c, ((0, 2 * SUBLANES - b), (0, 0)))
    mod = _modulation(c_pad, ada_w, ada_b)[:, :b].reshape(depth, b, 6, 1, d)
    cos, sina, sinb = _rope_tables(positions)

    w_in_p = _pack_w_in(w_in)
    w_out_b = w_out.astype(BF16)
    w_up_b = w_up.astype(BF16)
    w_down_b = w_down.astype(BF16)
    bg = jnp.pad(b_gates, ((0, 0), (0, LANES - 2 * nh))).reshape(depth, 1, LANES)
    norm_mix = norm_mix_g.reshape(depth, 1, d)
    norm_ffn = norm_ffn_g.reshape(depth, 1, d)
    head_g = mlstm_head_g.reshape(depth, 1, V_W)
    sink_rows = _sink_rows(sinks)
    gate_sel = _gate_selector()
    final_g2 = final_g.reshape(1, d)

    for l in range(depth):
        qk, v, o_sig, cols, qa, ka, va = _inproj(
            l, x, mod, norm_mix, w_in_p, conv_qk, bg, cos, sina, sinb)
        ym = _mlstm(l, qk, v, o_sig, cols, head_g, gate_sel)
        ya = _attention(l, qa, ka, va, sink_rows)
        x = _outproj(l, ym, ya, x, mod, w_out_b)
        x = _ffn(l, x, mod, norm_ffn, final_g2, w_up_b, conv_ffn, w_down_b,
                 final_norm=(l == depth - 1))
    return x
```
